```python
import math
import jax, jax.numpy as jnp
from jax import lax
import numpy as np

D_MODEL = 2048
BATCH = 4
SEQ = 2048
DEPTH = 1

Q_BLOCK = 128
DA_HEADS = 8
DA_HALF_DIM = 64
DA_V_DIM = 2 * DA_HALF_DIM
DA_WIDTH = DA_HEADS * DA_V_DIM
SB_HEADS = 8
SB_DIM = 128
SB_WIDTH = SB_HEADS * SB_DIM
REL_BUCKETS = 32
REL_MAX_DIST = 128
IN_COLS = 3 * DA_WIDTH + 3 * SB_WIDTH + 2 * D_MODEL
N_GROUPS = 8
EXPERTS_PER_GROUP = 8
N_EXPERTS = N_GROUPS * EXPERTS_PER_GROUP
TOP_K = 2
D_EXPERT = 1024
MOE_BLOCK = 128
N_MOD = 6
EPS = 1e-6

kernel_name = "hybrid_diffattn_stickbreak_hmoe_block"


def rms_norm(x, g):
    xf = x.astype(jnp.float32)
    y = xf * lax.rsqrt(jnp.mean(xf * xf, axis=-1, keepdims=True) + EPS)
    return (y * g.astype(jnp.float32)).astype(x.dtype)


def rel_bucket(q_pos, k_pos):
    n = jnp.maximum(q_pos[:, None] - k_pos[None, :], 0)
    max_exact = REL_BUCKETS // 2
    nf = jnp.maximum(n, 1).astype(jnp.float32)
    large = max_exact + (jnp.log(nf / max_exact) / math.log(REL_MAX_DIST / max_exact)
                         * (REL_BUCKETS - max_exact)).astype(jnp.int32)
    large = jnp.minimum(large, REL_BUCKETS - 1)
    return jnp.where(n < max_exact, n, large)


def split_heads(t, n_heads, dim):
    b, s, _ = t.shape
    return t.reshape(b, s, n_heads, dim).transpose(0, 2, 1, 3)


def merge_heads(t):
    b, h, s, d = t.shape
    return t.transpose(0, 2, 1, 3).reshape(b, s, h * d)


def to_blocks(t):
    b, h, s, d = t.shape
    return t.reshape(b, h, s // Q_BLOCK, Q_BLOCK, d).transpose(2, 0, 1, 3, 4)


def from_blocks(t):
    nqb, b, h, qb, d = t.shape
    return t.transpose(1, 2, 0, 3, 4).reshape(b, h, nqb * qb, d)


def diff_attention(q1, q2, k1, k2, v, lam, rel_table):
    s_len = q1.shape[2]
    k_pos = jnp.arange(s_len)
    scale = DA_HALF_DIM ** -0.5

    def block(args):
        q1b, q2b, blk = args
        q_pos = blk * Q_BLOCK + jnp.arange(Q_BLOCK)
        bias = rel_table[rel_bucket(q_pos, k_pos)].astype(jnp.float32).transpose(2, 0, 1)
        causal = k_pos[None, :] <= q_pos[:, None]

        def probs(qb, k):
            s = jnp.einsum('bhqd,bhkd->bhqk', qb, k).astype(jnp.float32) * scale + bias
            return jax.nn.softmax(jnp.where(causal, s, -jnp.inf), axis=-1)

        a = probs(q1b, k1) - lam * probs(q2b, k2)
        return jnp.einsum('bhqk,bhkd->bhqd', a.astype(v.dtype), v)

    out = lax.map(block, (to_blocks(q1), to_blocks(q2), jnp.arange(s_len // Q_BLOCK)))
    return from_blocks(out)


def stick_breaking_attention(q, k, v):
    s_len = q.shape[2]
    k_pos = jnp.arange(s_len)
    scale = SB_DIM ** -0.5

    def block(args):
        qb, blk = args
        q_pos = blk * Q_BLOCK + jnp.arange(Q_BLOCK)
        past = k_pos[None, :] < q_pos[:, None]
        z = jnp.einsum('bhqd,bhkd->bhqk', qb, k).astype(jnp.float32) * scale
        log_beta = jax.nn.log_sigmoid(z)
        log_keep = jnp.where(past, jax.nn.log_sigmoid(-z), 0.0)
        later = lax.cumsum(log_keep, axis=3, reverse=True) - log_keep
        w = jnp.where(past, jnp.exp(log_beta + later), 0.0)
        return jnp.einsum('bhqk,bhkd->bhqd', w.astype(v.dtype), v)

    out = lax.map(block, (to_blocks(q), jnp.arange(s_len // Q_BLOCK)))
    return from_blocks(out)


def hierarchical_moe(h, w_rg, b_rg, w_re, b_re, w_gate, w_up, w_down):
    bsz, slen, d = h.shape
    n_tok = bsz * slen
    hf = h.reshape(n_tok, d)
    g_logits = (hf @ w_rg).astype(jnp.float32) + b_rg.astype(jnp.float32)
    g_prob = jax.nn.softmax(g_logits, axis=-1)
    g_idx = jnp.argmax(g_logits, axis=-1)
    p_g = jnp.take_along_axis(g_prob, g_idx[:, None], axis=1)
    e_logits = ((hf @ w_re).astype(jnp.float32) + b_re.astype(jnp.float32)
                ).reshape(n_tok, N_GROUPS, EXPERTS_PER_GROUP)
    e_sel = jnp.take_along_axis(e_logits, g_idx[:, None, None], axis=1)[:, 0]
    top_p, top_i = lax.top_k(jax.nn.softmax(e_sel, axis=-1), TOP_K)
    top_p = top_p / jnp.sum(top_p, axis=-1, keepdims=True)
    gate = (p_g * top_p).reshape(-1)
    flat_e = (g_idx[:, None] * EXPERTS_PER_GROUP + top_i).reshape(-1)
    flat_tok = jnp.repeat(jnp.arange(n_tok, dtype=jnp.int32), TOP_K)
    m = n_tok * TOP_K
    n_blocks = -(-m // MOE_BLOCK) + N_EXPERTS
    rows = n_blocks * MOE_BLOCK
    onehot = jax.nn.one_hot(flat_e, N_EXPERTS, dtype=jnp.int32)
    rank = jnp.take_along_axis(jnp.cumsum(onehot, axis=0), flat_e[:, None], axis=1)[:, 0] - 1
    counts = jnp.sum(onehot, axis=0)
    padded = ((counts + MOE_BLOCK - 1) // MOE_BLOCK) * MOE_BLOCK
    pend = jnp.cumsum(padded)
    dest = (pend - padded)[flat_e] + rank
    buf_tok = jnp.zeros((rows,), jnp.int32).at[dest].set(flat_tok)
    buf_w = jnp.zeros((rows,), jnp.float32).at[dest].set(gate)
    starts = jnp.arange(n_blocks) * MOE_BLOCK
    block_e = jnp.minimum(jnp.sum(pend[None, :] <= starts[:, None], axis=1), N_EXPERTS - 1)
    xs = hf[buf_tok].reshape(n_blocks, MOE_BLOCK, d)

    def expert_block(args):
        xb, e = args
        a = xb @ w_gate[e]
        u = xb @ w_up[e]
        return (jax.nn.silu(a) * u) @ w_down[e]

    ys = lax.map(expert_block, (xs, block_e)).reshape(rows, d)
    out = jnp.zeros((n_tok, d), h.dtype).at[buf_tok].add(ys * buf_w[:, None].astype(h.dtype))
    return out.reshape(bsz, slen, d)


def setup_inputs(seed: int = 0) -> dict:
    key = jax.random.key(seed)
    ks = jax.random.split(key, 24)
    f32 = jnp.float32
    nrm = lambda k, shape, s: jax.random.normal(k, shape, f32) * s
    D = D_MODEL
    return {
        "x": nrm(ks[0], (BATCH, SEQ, D), 1.0),
        "c": nrm(ks[1], (BATCH, D), 1.0),
        "rel_bias_table": nrm(ks[2], (REL_BUCKETS, DA_HEADS), 0.5),
        "w_ada": nrm(ks[3], (DEPTH, D, N_MOD * D), D ** -0.5),
        "b_ada": nrm(ks[4], (DEPTH, N_MOD * D), 0.02),
        "g_mix": 1.0 + nrm(ks[5], (DEPTH, D), 0.02),
        "w_in": nrm(ks[6], (DEPTH, D, IN_COLS), D ** -0.5),
        "lambda_q1": nrm(ks[7], (DEPTH, DA_HALF_DIM), 0.1),
        "lambda_k1": nrm(ks[8], (DEPTH, DA_HALF_DIM), 0.1),
        "lambda_q2": nrm(ks[9], (DEPTH, DA_HALF_DIM), 0.1),
        "lambda_k2": nrm(ks[10], (DEPTH, DA_HALF_DIM), 0.1),
        "g_subln": 1.0 + nrm(ks[11], (DEPTH, DA_V_DIM), 0.02),
        "w_proj_a": nrm(ks[12], (DEPTH, DA_WIDTH, D), DA_WIDTH ** -0.5),
        "w_proj_b": nrm(ks[13], (DEPTH, SB_WIDTH, D), SB_WIDTH ** -0.5),
        "w_out": nrm(ks[14], (DEPTH, D, D), D ** -0.5),
        "g_ffn": 1.0 + nrm(ks[15], (DEPTH, D), 0.02),
        "w_router_group": nrm(ks[16], (DEPTH, D, N_GROUPS), D ** -0.5),
        "b_router_group": nrm(ks[17], (DEPTH, N_GROUPS), 0.01),
        "w_router_expert": nrm(ks[18], (DEPTH, D, N_EXPERTS), D ** -0.5),
        "b_router_expert": nrm(ks[19], (DEPTH, N_EXPERTS), 0.01),
        "w_expert_gate": nrm(ks[20], (DEPTH, N_EXPERTS, D, D_EXPERT), D ** -0.5),
        "w_expert_up": nrm(ks[21], (DEPTH, N_EXPERTS, D, D_EXPERT), D ** -0.5),
        "w_expert_down": nrm(ks[22], (DEPTH, N_EXPERTS, D_EXPERT, D), D_EXPERT ** -0.5),
        "g_final": 1.0 + nrm(ks[23], (D,), 0.02),
    }


def reference(x, c, rel_bias_table, w_ada, b_ada, g_mix, w_in, lambda_q1, lambda_k1,
              lambda_q2, lambda_k2, g_subln, w_proj_a, w_proj_b, w_out, g_ffn,
              w_router_group, b_router_group, w_router_expert, b_router_expert,
              w_expert_gate, w_expert_up, w_expert_down, g_final):
    D = D_MODEL
    cut = np.cumsum([DA_WIDTH, DA_WIDTH, DA_WIDTH, SB_WIDTH, SB_WIDTH, SB_WIDTH, D]).tolist()
    for l in range(DEPTH):
        lam_init = 0.8 - 0.6 * math.exp(-0.3 * l)
        mod = jax.nn.silu(c) @ w_ada[l] + b_ada[l]
        shift_m, scale_m, gate_m, shift_f, scale_f, gate_f = jnp.split(mod[:, None, :], N_MOD, axis=-1)

        h = rms_norm(x, g_mix[l]) * (1.0 + scale_m) + shift_m
        proj = h @ w_in[l]
        qa, ka, va, qb, kb, vb, gate_a, gate_b = jnp.split(proj, cut, axis=-1)
        qa = split_heads(qa, DA_HEADS, 2 * DA_HALF_DIM)
        ka = split_heads(ka, DA_HEADS, 2 * DA_HALF_DIM)
        va = split_heads(va, DA_HEADS, DA_V_DIM)
        lam = (jnp.exp(jnp.sum(lambda_q1[l].astype(jnp.float32) * lambda_k1[l].astype(jnp.float32)))
               - jnp.exp(jnp.sum(lambda_q2[l].astype(jnp.float32) * lambda_k2[l].astype(jnp.float32)))
               + lam_init)
        oa = diff_attention(qa[..., :DA_HALF_DIM], qa[..., DA_HALF_DIM:],
                            ka[..., :DA_HALF_DIM], ka[..., DA_HALF_DIM:], va, lam, rel_bias_table)
        oa = merge_heads(rms_norm(oa, g_subln[l]) * (1.0 - lam_init))
        ob = merge_heads(stick_breaking_attention(split_heads(qb, SB_HEADS, SB_DIM),
                                                  split_heads(kb, SB_HEADS, SB_DIM),
                                                  split_heads(vb, SB_HEADS, SB_DIM)))
        merged = (jax.nn.sigmoid(gate_a) * (oa @ w_proj_a[l])
                  + jax.nn.sigmoid(gate_b) * (ob @ w_proj_b[l]))
        x = x + gate_m * (merged @ w_out[l])

        h2 = rms_norm(x, g_ffn[l]) * (1.0 + scale_f) + shift_f
        x = x + gate_f * hierarchical_moe(h2, w_router_group[l], b_router_group[l],
                                          w_router_expert[l], b_router_expert[l],
                                          w_expert_gate[l], w_expert_up[l], w_expert_down[l])
    return rms_norm(x, g_final)
```

```python
import functools
import math

import jax
import jax.numpy as jnp
from jax import lax
from jax.experimental import pallas as pl
from jax.experimental.pallas import tpu as pltpu

F32 = jnp.float32
BF16 = jnp.bfloat16
I32 = jnp.int32
EPS = 1e-6

DA_HEADS = 8
DA_HALF_DIM = 64
SB_HEADS = 8
HEAD_DIM = 128
REL_BUCKETS = 32
REL_MAX_DIST = 128
N_GROUPS = 8
EXPERTS_PER_GROUP = 8
N_MOD = 6

VMEM_LIMIT_BYTES = 56 * 1024 * 1024
LANES = 128

NT_DIMS = (((1,), (1,)), ((), ()))


def _params(*sem):
    return pltpu.CompilerParams(dimension_semantics=sem, vmem_limit_bytes=VMEM_LIMIT_BYTES)


def _sigmoid(v):
    return 1.0 / (1.0 + jnp.exp(-v))


def _mod_kernel(c_ref, w_ref, b_ref, o_ref):
    c = c_ref[...]
    s = (c * _sigmoid(c)).astype(BF16)
    o_ref[...] = jnp.dot(s, w_ref[...].astype(BF16), preferred_element_type=F32) + b_ref[...]


def _adaln_mod(c, w_ada, b_ada):
    bsz, d = c.shape
    ncol = w_ada.shape[-1]
    tn = 1024
    return pl.pallas_call(
        _mod_kernel,
        out_shape=jax.ShapeDtypeStruct((bsz, ncol), F32),
        grid=(ncol // tn,),
        in_specs=[pl.BlockSpec((bsz, d), lambda j: (0, 0)),
                  pl.BlockSpec((None, d, tn), lambda j: (0, 0, j)),
                  pl.BlockSpec((1, tn), lambda j: (0, j))],
        out_specs=pl.BlockSpec((bsz, tn), lambda j: (0, j)),
        compiler_params=_params("arbitrary"),
        name="adaln_mod",
    )(c, w_ada, b_ada.reshape(1, ncol))


def _hnorm_kernel(x_ref, g_ref, sc_ref, sh_ref, o_ref):
    x = x_ref[...]
    ms = jnp.mean(x * x, axis=-1, keepdims=True)
    y = x * lax.rsqrt(ms + EPS) * g_ref[...]
    o_ref[...] = (y * (1.0 + sc_ref[...]) + sh_ref[...]).astype(o_ref.dtype)


def _norm_modulate(x, g, mod3, scale_idx, shift_idx):
    bsz, s, d = x.shape
    ts = 512
    return pl.pallas_call(
        _hnorm_kernel,
        out_shape=jax.ShapeDtypeStruct((bsz, s, d), BF16),
        grid=(bsz, s // ts),
        in_specs=[pl.BlockSpec((None, ts, d), lambda b, i: (b, i, 0)),
                  pl.BlockSpec((1, d), lambda b, i: (0, 0)),
                  pl.BlockSpec((None, 1, d), lambda b, i: (b, 0, scale_idx)),
                  pl.BlockSpec((None, 1, d), lambda b, i: (b, 0, shift_idx))],
        out_specs=pl.BlockSpec((None, ts, d), lambda b, i: (b, i, 0)),
        compiler_params=_params("arbitrary", "arbitrary"),
        name="norm_modulate",
    )(x, g.reshape(1, d), mod3, mod3)


def _proj_kernel(h_ref, w_ref, o_ref, wb_ref, *, gate):
    @pl.when(pl.program_id(1) == 0)
    def _():
        wb_ref[...] = w_ref[...].astype(BF16)

    r = jnp.dot(h_ref[...], wb_ref[...], preferred_element_type=F32)
    if gate:
        r = _sigmoid(r)
    o_ref[...] = r.astype(o_ref.dtype)


def _in_proj(h2d, w_in, col0, ncols, out_dtype, gate):
    n, d = h2d.shape
    tn, tm = 1024, 1024
    jb = col0 // tn
    return pl.pallas_call(
        functools.partial(_proj_kernel, gate=gate),
        out_shape=jax.ShapeDtypeStruct((n, ncols), out_dtype),
        grid=(ncols // tn, n // tm),
        in_specs=[pl.BlockSpec((tm, d), lambda j, i: (i, 0)),
                  pl.BlockSpec((None, d, tn), lambda j, i: (0, 0, j + jb))],
        out_specs=pl.BlockSpec((tm, tn), lambda j, i: (i, j)),
        scratch_shapes=[pltpu.VMEM((d, tn), BF16)],
        compiler_params=_params("arbitrary", "arbitrary"),
        name="in_proj_gate" if gate else "in_proj_qkv",
    )(h2d, w_in)


def _rel_bucket(n):
    n = jnp.maximum(n, 0)
    max_exact = REL_BUCKETS // 2
    nf = jnp.maximum(n, 1).astype(F32)
    large = max_exact + (jnp.log(nf / max_exact) / math.log(REL_MAX_DIST / max_exact)
                         * (REL_BUCKETS - max_exact)).astype(I32)
    large = jnp.minimum(large, REL_BUCKETS - 1)
    return jnp.where(n < max_exact, n, large)


def _bias_tiles(rel_table, t):
    assert 2 * t - (t - 1) >= REL_MAX_DIST, "far tiles must sit wholly in the last bucket"
    r = jnp.arange(t, dtype=I32)[:, None]
    c = jnp.arange(t, dtype=I32)[None, :]
    tiles = []
    for delta in range(3):
        b = rel_table[_rel_bucket(delta * t + r - c)].astype(F32)
        if delta == 0:
            b = jnp.where((c <= r)[..., None], b, -jnp.inf)
        tiles.append(b)
    return jnp.stack(tiles).transpose(3, 0, 1, 2)


def _diff_attn_kernel(lam_ref, g_ref, bias_ref, q_ref, k_ref, v_ref, o_ref, *, t, lam_init):
    qi = pl.program_id(2)
    lp = lam_ref[...]
    lam = (jnp.exp(jnp.sum(lp[0:1] * lp[1:2], axis=-1, keepdims=True))
           - jnp.exp(jnp.sum(lp[2:3] * lp[3:4], axis=-1, keepdims=True)) + lam_init)

    q = q_ref[...] * jnp.asarray(DA_HALF_DIM ** -0.5, BF16)
    lane = lax.broadcasted_iota(I32, q.shape, 1)
    q1 = jnp.where(lane < DA_HALF_DIM, q, jnp.zeros_like(q))
    q2 = jnp.where(lane >= DA_HALF_DIM, q, jnp.zeros_like(q))

    def online(qh, k, v, bias, m, l, acc):
        s = lax.dot_general(qh, k, NT_DIMS, preferred_element_type=F32) + bias
        mn = jnp.maximum(m, jnp.max(s, axis=-1, keepdims=True))
        alpha = jnp.exp(m - mn)
        p = jnp.exp(s - mn)
        l = alpha * l + jnp.sum(p, axis=-1, keepdims=True)
        acc = alpha * acc + jnp.dot(p.astype(BF16), v, preferred_element_type=F32)
        return mn, l, acc

    def body(ki, carry):
        m1, l1, a1, m2, l2, a2 = carry
        ks = pl.multiple_of(ki * t, t)
        k = k_ref[pl.ds(ks, t), :]
        v = v_ref[pl.ds(ks, t), :]
        bias = bias_ref[jnp.minimum(qi - ki, 2)]
        m1, l1, a1 = online(q1, k, v, bias, m1, l1, a1)
        m2, l2, a2 = online(q2, k, v, bias, m2, l2, a2)
        return m1, l1, a1, m2, l2, a2

    m0 = jnp.full((t, 1), -1e30, F32)
    l0 = jnp.zeros((t, 1), F32)
    a0 = jnp.zeros((t, HEAD_DIM), F32)
    _, l1, a1, _, l2, a2 = lax.fori_loop(0, qi + 1, body, (m0, l0, a0, m0, l0, a0))

    o = a1 / l1 - lam * (a2 / l2)
    ms = jnp.mean(o * o, axis=-1, keepdims=True)
    o = (o * lax.rsqrt(ms + EPS) * g_ref[...]) * (1.0 - lam_init)
    o_ref[...] = o.astype(o_ref.dtype)


def _diff_attention(qkv3, lam_params, g_subln, bias, lam_init, t):
    bsz, s, _ = qkv3.shape
    h = DA_HEADS
    return pl.pallas_call(
        functools.partial(_diff_attn_kernel, t=t, lam_init=lam_init),
        out_shape=jax.ShapeDtypeStruct((bsz, s, h * HEAD_DIM), BF16),
        grid=(bsz, h, s // t),
        in_specs=[pl.BlockSpec((4, DA_HALF_DIM), lambda b, hh, i: (0, 0)),
                  pl.BlockSpec((1, HEAD_DIM), lambda b, hh, i: (0, 0)),
                  pl.BlockSpec((None, 3, t, t), lambda b, hh, i: (hh, 0, 0, 0)),
                  pl.BlockSpec((None, t, HEAD_DIM), lambda b, hh, i: (b, i, hh)),
                  pl.BlockSpec((None, s, HEAD_DIM), lambda b, hh, i: (b, 0, h + hh)),
                  pl.BlockSpec((None, s, HEAD_DIM), lambda b, hh, i: (b, 0, 2 * h + hh))],
        out_specs=pl.BlockSpec((None, t, HEAD_DIM), lambda b, hh, i: (b, i, hh)),
        compiler_params=_params("arbitrary", "arbitrary", "arbitrary"),
        name="diff_attention",
    )(lam_params, g_subln.reshape(1, HEAD_DIM), bias, qkv3, qkv3, qkv3)


def _sb_attn_kernel(q_ref, k_ref, v_ref, o_ref, *, t, scale):
    qi = pl.program_id(2)
    q = q_ref[...]
    row = lax.broadcasted_iota(I32, (t, t), 0)
    col = lax.broadcasted_iota(I32, (t, t), 1)
    tri = (row >= col).astype(BF16)
    past = col < row

    def tile(ki, acc, cs, diag):
        ks = pl.multiple_of(ki * t, t)
        k = k_ref[pl.ds(ks, t), :]
        v = v_ref[pl.ds(ks, t), :]
        z = lax.dot_general(q, k, NT_DIMS, preferred_element_type=F32) * scale
        log_beta = jnp.minimum(z, 0.0) - jnp.log(1.0 + jnp.exp(-jnp.abs(z)))
        log_keep = log_beta - z
        if diag:
            log_keep = jnp.where(past, log_keep, 0.0)
        hi = log_keep.astype(BF16)
        lo = (log_keep - hi.astype(F32)).astype(BF16)
        incl = (jnp.dot(hi, tri, preferred_element_type=F32)
                + jnp.dot(lo, tri, preferred_element_type=F32))
        w = jnp.exp(z + incl + cs)
        if diag:
            w = jnp.where(past, w, 0.0)
        acc = acc + jnp.dot(w.astype(BF16), v, preferred_element_type=F32)
        return acc, cs + incl[:, 0:1]

    acc0 = jnp.zeros((t, HEAD_DIM), F32)
    cs0 = jnp.zeros((t, 1), F32)
    acc, cs = tile(qi, acc0, cs0, True)

    def body(j, carry):
        return tile(qi - 1 - j, carry[0], carry[1], False)

    acc, _ = lax.fori_loop(0, qi, body, (acc, cs))
    o_ref[...] = acc.astype(o_ref.dtype)


def _sb_attention(qkv3, col_block0, t):
    bsz, s, _ = qkv3.shape
    h = SB_HEADS
    return pl.pallas_call(
        functools.partial(_sb_attn_kernel, t=t, scale=HEAD_DIM ** -0.5),
        out_shape=jax.ShapeDtypeStruct((bsz, s, h * HEAD_DIM), BF16),
        grid=(bsz, h, s // t),
        in_specs=[pl.BlockSpec((None, t, HEAD_DIM), lambda b, hh, i: (b, i, col_block0 + hh)),
                  pl.BlockSpec((None, s, HEAD_DIM), lambda b, hh, i: (b, 0, col_block0 + h + hh)),
                  pl.BlockSpec((None, s, HEAD_DIM), lambda b, hh, i: (b, 0, col_block0 + 2 * h + hh))],
        out_specs=pl.BlockSpec((None, t, HEAD_DIM), lambda b, hh, i: (b, i, hh)),
        compiler_params=_params("arbitrary", "arbitrary", "arbitrary"),
        name="sb_attention",
    )(qkv3, qkv3, qkv3)


def _post_kernel(oa_ref, ob_ref, sa_ref, sb_ref, x_ref, gm_ref, scf_ref, shf_ref, gffn_ref,
                 wa_ref, wb_ref, wo_ref, wrh_ref, wrl_ref, br_ref,
                 x1_ref, h2_ref, lg_ref):
    merged = (sa_ref[...] * jnp.dot(oa_ref[...], wa_ref[...], preferred_element_type=F32)
              + sb_ref[...] * jnp.dot(ob_ref[...], wb_ref[...], preferred_element_type=F32))
    y = jnp.dot(merged.astype(BF16), wo_ref[...], preferred_element_type=F32)
    x1 = x_ref[...] + gm_ref[...] * y
    x1_ref[...] = x1
    ms = jnp.mean(x1 * x1, axis=-1, keepdims=True)
    h2 = (x1 * lax.rsqrt(ms + EPS) * gffn_ref[...]) * (1.0 + scf_ref[...]) + shf_ref[...]
    h2_ref[...] = h2
    hb = h2.astype(BF16)
    hl = (h2 - hb.astype(F32)).astype(BF16)
    wrh = wrh_ref[...]
    lg = (lax.dot_general(wrh, hb, NT_DIMS, preferred_element_type=F32)
          + lax.dot_general(wrh, hl, NT_DIMS, preferred_element_type=F32)
          + lax.dot_general(wrl_ref[...], hb, NT_DIMS, preferred_element_type=F32))
    lg_ref[...] = lg + br_ref[...]


def _post_attention(oa, ob, gates, x2d, mod3, g_ffn, wa, wb, wo, wrh, wrl, br, seq):
    n, d = x2d.shape
    wa_rows = oa.shape[1]
    tm = 256
    per_b = seq // tm
    const = lambda shape: pl.BlockSpec(shape, lambda i: (0,) * len(shape), pipeline_mode=pl.Buffered(1))
    modspec = lambda idx: pl.BlockSpec((None, 1, d), lambda i: (i // per_b, 0, idx))
    return pl.pallas_call(
        _post_kernel,
        out_shape=(jax.ShapeDtypeStruct((n, d), F32),
                   jax.ShapeDtypeStruct((n, d), F32),
                   jax.ShapeDtypeStruct((LANES, n), F32)),
        grid=(n // tm,),
        in_specs=[pl.BlockSpec((tm, wa_rows), lambda i: (i, 0)),
                  pl.BlockSpec((tm, wa_rows), lambda i: (i, 0)),
                  pl.BlockSpec((tm, d), lambda i: (i, 0)),
                  pl.BlockSpec((tm, d), lambda i: (i, 1)),
                  pl.BlockSpec((tm, d), lambda i: (i, 0)),
                  modspec(2), modspec(4), modspec(3),
                  const((1, d)),
                  const((wa_rows, d)), const((wa_rows, d)), const((d, d)),
                  const((LANES, d)), const((LANES, d)), const((LANES, 1))],
        out_specs=(pl.BlockSpec((tm, d), lambda i: (i, 0)),
                   pl.BlockSpec((tm, d), lambda i: (i, 0)),
                   pl.BlockSpec((LANES, tm), lambda i: (0, i))),
        compiler_params=_params("arbitrary"),
        name="post_attention",
    )(oa, ob, gates, gates, x2d, mod3, mod3, mod3, g_ffn.reshape(1, d), wa, wb, wo, wrh, wrl, br)


def _first_index_of_max(vals, iota, nrows):
    mx = jnp.max(vals, axis=0, keepdims=True)
    idx = jnp.min(jnp.where(vals == mx, iota, nrows), axis=0, keepdims=True)
    return mx, idx


def _route_kernel(lg_ref, e_ref, w_ref):
    g = N_GROUPS
    epg = EXPERTS_PER_GROUP
    lg = lg_ref[...]
    gl = lg[0:g, :]
    iota = lax.broadcasted_iota(I32, gl.shape, 0)
    gmax, gidx = _first_index_of_max(gl, iota, g)
    p_g = 1.0 / jnp.sum(jnp.exp(gl - gmax), axis=0, keepdims=True)

    esel = jnp.zeros((epg, lg.shape[1]), F32)
    for gi in range(g):
        esel = jnp.where(gidx == gi, lg[g + gi * epg:g + (gi + 1) * epg, :], esel)
    emax = jnp.max(esel, axis=0, keepdims=True)
    ex = jnp.exp(esel - emax)
    prob = ex / jnp.sum(ex, axis=0, keepdims=True)

    p0, i0 = _first_index_of_max(prob, iota, epg)
    rest = jnp.where(iota == i0, -1.0, prob)
    p1, i1 = _first_index_of_max(rest, iota, epg)
    tot = p0 + p1
    e_ref[0:1, :] = gidx * epg + i0
    e_ref[1:2, :] = gidx * epg + i1
    w_ref[0:1, :] = p_g * (p0 / tot)
    w_ref[1:2, :] = p_g * (p1 / tot)


def _route(logits_t):
    rows, n = logits_t.shape
    tn = 1024
    return pl.pallas_call(
        _route_kernel,
        out_shape=(jax.ShapeDtypeStruct((2, n), I32), jax.ShapeDtypeStruct((2, n), F32)),
        grid=(n // tn,),
        in_specs=[pl.BlockSpec((rows, tn), lambda i: (0, i))],
        out_specs=(pl.BlockSpec((2, tn), lambda i: (0, i)), pl.BlockSpec((2, tn), lambda i: (0, i))),
        compiler_params=_params("arbitrary"),
        name="route",
    )(logits_t)


VISIT_ROWS = 512
MOE_TILE = 256
MOE_FCHUNK = 256


def _moe_kernel(ve_ref, vnt_ref,
                tokc_ref, tokn_ref, h_hbm, wg_ref, wu_ref, wd_ref,
                y_ref,
                xbuf, sem, wgb, wub, wdb, *, nf, nv):
    v = pl.program_id(0)
    f = pl.program_id(1)
    slot = v % 2
    nt = vnt_ref[v]
    part = VISIT_ROWS // nf

    def row_copy(tok_ref, dst_slot, r):
        tok = tok_ref[0, r]
        return pltpu.make_async_copy(h_hbm.at[pl.ds(tok, 1)], xbuf.at[dst_slot, pl.ds(r, 1)],
                                     sem.at[dst_slot])

    def issue(tok_ref, dst_slot, r0, nrows):
        def body(r, c):
            row_copy(tok_ref, dst_slot, r0 + r).start()
            return c
        lax.fori_loop(0, nrows, body, 0, unroll=8)

    @pl.when(jnp.logical_and(v == 0, f == 0))
    def _():
        for tl in range(VISIT_ROWS // MOE_TILE):
            @pl.when(tl < nt)
            def _():
                issue(tokc_ref, 0, tl * MOE_TILE, MOE_TILE)

    @pl.when(jnp.logical_and(f == 0, nt > 0))
    def _():
        for tl in range(VISIT_ROWS // MOE_TILE):
            @pl.when(tl < nt)
            def _():
                def body(r, c):
                    row_copy(tokc_ref, slot, tl * MOE_TILE + r).wait()
                    return c
                lax.fori_loop(0, MOE_TILE, body, 0, unroll=8)

    @pl.when(f == 0)
    def _():
        y_ref[...] = jnp.zeros_like(y_ref)

    nt_next = jnp.where(v + 1 < nv, vnt_ref[jnp.minimum(v + 1, nv - 1)], 0)

    @pl.when(nt_next * MOE_TILE > f * part)
    def _():
        issue(tokn_ref, 1 - slot, f * part, part)

    @pl.when(nt > 0)
    def _():
        wgb[...] = wg_ref[...].astype(BF16)
        wub[...] = wu_ref[...].astype(BF16)
        wdb[...] = wd_ref[...].astype(BF16)
        for tl in range(VISIT_ROWS // MOE_TILE):
            @pl.when(tl < nt)
            def _():
                rows = pl.ds(tl * MOE_TILE, MOE_TILE)
                x = xbuf[slot, rows, :].astype(BF16)
                a = jnp.dot(x, wgb[...], preferred_element_type=F32)
                u = jnp.dot(x, wub[...], preferred_element_type=F32)
                hmid = ((a * _sigmoid(a)) * u).astype(BF16)
                y_ref[rows, :] += jnp.dot(hmid, wdb[...], preferred_element_type=F32)


def _moe_experts(h2, row_tok, vis_e, vis_nt, w_gate, w_up, w_down):
    n, d = h2.shape
    n_exp, _, dexp = w_gate.shape
    nv = vis_e.shape[0]
    nf = dexp // MOE_FCHUNK
    tok3 = row_tok.reshape(nv, 1, VISIT_ROWS)

    def fidx(v, f, vnt):
        return jnp.where(vnt[v] > 0, f, nf - 1)

    grid_spec = pltpu.PrefetchScalarGridSpec(
        num_scalar_prefetch=2,
        grid=(nv, nf),
        in_specs=[
            pl.BlockSpec((None, 1, VISIT_ROWS), lambda v, f, ve, vnt: (v, 0, 0),
                         memory_space=pltpu.SMEM),
            pl.BlockSpec((None, 1, VISIT_ROWS), lambda v, f, ve, vnt: (jnp.minimum(v + 1, nv - 1), 0, 0),
                         memory_space=pltpu.SMEM),
            pl.BlockSpec(memory_space=pl.ANY),
            pl.BlockSpec((None, d, MOE_FCHUNK), lambda v, f, ve, vnt: (ve[v], 0, fidx(v, f, vnt))),
            pl.BlockSpec((None, d, MOE_FCHUNK), lambda v, f, ve, vnt: (ve[v], 0, fidx(v, f, vnt))),
            pl.BlockSpec((None, MOE_FCHUNK, d), lambda v, f, ve, vnt: (ve[v], fidx(v, f, vnt), 0)),
        ],
        out_specs=pl.BlockSpec((VISIT_ROWS, d), lambda v, f, ve, vnt: (v, 0)),
        scratch_shapes=[pltpu.VMEM((2, VISIT_ROWS, d), F32),
                        pltpu.SemaphoreType.DMA((2,)),
                        pltpu.VMEM((d, MOE_FCHUNK), BF16),
                        pltpu.VMEM((d, MOE_FCHUNK), BF16),
                        pltpu.VMEM((MOE_FCHUNK, d), BF16)],
    )
    return pl.pallas_call(
        functools.partial(_moe_kernel, nf=nf, nv=nv),
        out_shape=jax.ShapeDtypeStruct((nv * VISIT_ROWS, d), F32),
        grid_spec=grid_spec,
        compiler_params=_params("arbitrary", "arbitrary"),
        name="moe_experts",
    )(vis_e, vis_nt, tok3, tok3, h2, w_gate, w_up, w_down)


def _expert_layout(e2, n_exp):
    n = e2.shape[1]
    m = 2 * n
    nv = m // VISIT_ROWS + n_exp
    flat_e = e2.T.reshape(-1)
    flat_tok = jnp.repeat(jnp.arange(n, dtype=I32), 2)
    onehot = (flat_e[:, None] == jnp.arange(n_exp, dtype=I32)[None, :]).astype(I32)
    csum = jnp.cumsum(onehot, axis=0)
    rank = jnp.take_along_axis(csum, flat_e[:, None], axis=1)[:, 0] - 1
    counts = csum[-1]
    n_vis = (counts + VISIT_ROWS - 1) // VISIT_ROWS
    cum_vis = jnp.cumsum(n_vis)
    vbase = cum_vis - n_vis
    dest = vbase[flat_e] * VISIT_ROWS + rank
    row_tok = jnp.zeros((nv * VISIT_ROWS,), I32).at[dest].set(flat_tok)

    n_used = cum_vis[-1]
    vid = jnp.arange(nv, dtype=I32)
    used = vid < n_used
    ve = jnp.minimum(jnp.searchsorted(cum_vis, vid, side="right").astype(I32), n_exp - 1)
    rem = counts[ve] - (vid - vbase[ve]) * VISIT_ROWS
    nt = jnp.clip((rem + MOE_TILE - 1) // MOE_TILE, 0, VISIT_ROWS // MOE_TILE)
    last = jnp.maximum(n_used - 1, 0)
    vis_nt = jnp.where(used, nt, 0).astype(I32)
    vis_e = jnp.where(used, ve, ve[last]).astype(I32)
    return dest.reshape(n, 2), row_tok, vis_e, vis_nt


def _final_kernel(posc_ref, posn_ref, x1_ref, gf_ref, w_ref, g_ref, ys_hbm, o_ref, ybuf, sem,
                  *, tm, nsteps):
    i = pl.program_id(0)
    slot = i % 2

    def row_copy(pos_ref, dst_slot, j):
        return pltpu.make_async_copy(ys_hbm.at[pl.ds(pos_ref[0, j], 1)],
                                     ybuf.at[dst_slot, pl.ds(j, 1)], sem.at[dst_slot])

    def issue(pos_ref, dst_slot):
        def body(j, c):
            row_copy(pos_ref, dst_slot, j).start()
            return c
        lax.fori_loop(0, 2 * tm, body, 0, unroll=8)

    @pl.when(i == 0)
    def _():
        issue(posc_ref, 0)

    @pl.when(i + 1 < nsteps)
    def _():
        issue(posn_ref, 1 - slot)

    def wbody(j, c):
        row_copy(posc_ref, slot, j).wait()
        return c
    lax.fori_loop(0, 2 * tm, wbody, 0, unroll=8)

    w = w_ref[...]
    moe = w[:, 0:1] * ybuf[slot, 0:tm, :] + w[:, 1:2] * ybuf[slot, tm:2 * tm, :]
    x = x1_ref[...] + gf_ref[...] * moe
    ms = jnp.mean(x * x, axis=-1, keepdims=True)
    o_ref[...] = x * lax.rsqrt(ms + EPS) * g_ref[...]


def _combine_final(x1, ys, pos, gate_w, mod3, g_final, seq):
    n, d = x1.shape
    tm = 256
    nsteps = n // tm
    per_b = seq // tm
    pos3 = pos.reshape(nsteps, tm, 2).transpose(0, 2, 1).reshape(nsteps, 1, 2 * tm)
    return pl.pallas_call(
        functools.partial(_final_kernel, tm=tm, nsteps=nsteps),
        out_shape=jax.ShapeDtypeStruct((n, d), F32),
        grid=(nsteps,),
        in_specs=[pl.BlockSpec((None, 1, 2 * tm), lambda i: (i, 0, 0), memory_space=pltpu.SMEM),
                  pl.BlockSpec((None, 1, 2 * tm), lambda i: (jnp.minimum(i + 1, nsteps - 1), 0, 0),
                               memory_space=pltpu.SMEM),
                  pl.BlockSpec((tm, d), lambda i: (i, 0)),
                  pl.BlockSpec((None, 1, d), lambda i: (i // per_b, 0, 5)),
                  pl.BlockSpec((tm, 2), lambda i: (i, 0)),
                  pl.BlockSpec((1, d), lambda i: (0, 0)),
                  pl.BlockSpec(memory_space=pl.ANY)],
        out_specs=pl.BlockSpec((tm, d), lambda i: (i, 0)),
        scratch_shapes=[pltpu.VMEM((2, 2 * tm, d), F32), pltpu.SemaphoreType.DMA((2,))],
        compiler_params=_params("arbitrary"),
        name="combine_final",
    )(pos3, pos3, x1, mod3, gate_w, g_final.reshape(1, d), ys)


def kernel(x, c, rel_bias_table, w_ada, b_ada, g_mix, w_in, lambda_q1, lambda_k1, lambda_q2,
           lambda_k2, g_subln, w_proj_a, w_proj_b, w_out, g_ffn, w_router_group, b_router_group,
           w_router_expert, b_router_expert, w_expert_gate, w_expert_up, w_expert_down, g_final):
    bsz, seq, d = x.shape
    n = bsz * seq
    depth = w_in.shape[0]
    assert depth == 1, "the MoE combine is fused with the final RMSNorm: one layer only"
    da_width = DA_HEADS * HEAD_DIM
    sb_width = SB_HEADS * HEAD_DIM
    qkv_cols = 3 * da_width + 3 * sb_width
    attn_tile = 256
    n_exp = w_expert_gate.shape[1]
    xf = x.reshape(n, d)

    for l in range(depth):
        lam_init = 0.8 - 0.6 * math.exp(-0.3 * l)
        mod = _adaln_mod(c, w_ada[l:l + 1], b_ada[l])
        mod3 = mod.reshape(bsz, 1, N_MOD * d)

        h = _norm_modulate(xf.reshape(bsz, seq, d), g_mix[l], mod3, 1, 0).reshape(n, d)
        qkv = _in_proj(h, w_in[l:l + 1], 0, qkv_cols, BF16, gate=False)
        gates = _in_proj(h, w_in[l:l + 1], qkv_cols, 2 * d, F32, gate=True)
        qkv3 = qkv.reshape(bsz, seq, qkv_cols)

        lam_params = jnp.stack([lambda_q1[l], lambda_k1[l], lambda_q2[l], lambda_k2[l]]).astype(F32)
        bias = _bias_tiles(rel_bias_table, attn_tile)
        oa = _diff_attention(qkv3, lam_params, g_subln[l], bias, lam_init, attn_tile)
        ob = _sb_attention(qkv3, 3 * DA_HEADS, attn_tile)

        w_r = jnp.concatenate([w_router_group[l], w_router_expert[l]], axis=1).astype(F32).T
        w_r = jnp.pad(w_r, ((0, LANES - w_r.shape[0]), (0, 0)))
        wrh = w_r.astype(BF16)
        wrl = (w_r - wrh.astype(F32)).astype(BF16)
        b_r = jnp.concatenate([b_router_group[l], b_router_expert[l]]).astype(F32)
        b_r = jnp.pad(b_r, (0, LANES - b_r.shape[0])).reshape(LANES, 1)
        x1, h2, logits_t = _post_attention(
            oa.reshape(n, da_width), ob.reshape(n, sb_width), gates, xf, mod3, g_ffn[l],
            w_proj_a[l].astype(BF16), w_proj_b[l].astype(BF16), w_out[l].astype(BF16),
            wrh, wrl, b_r, seq)

        e2, g2 = _route(logits_t)
        pos, row_tok, vis_e, vis_nt = _expert_layout(e2, n_exp)
        ys = _moe_experts(h2, row_tok, vis_e, vis_nt,
                          w_expert_gate[l], w_expert_up[l], w_expert_down[l])
        out = _combine_final(x1, ys, pos, g2.T, mod3, g_final, seq)
    return out.reshape(bsz, seq, d)
```

```python
import functools
import math

import jax
import jax.numpy as jnp
from jax import lax
from jax.experimental import pallas as pl
from jax.experimental.pallas import tpu as pltpu

F32 = jnp.float32
BF16 = jnp.bfloat16
I32 = jnp.int32
EPS = 1e-6

DA_HEADS = 8
DA_HALF_DIM = 64
SB_HEADS = 8
HEAD_DIM = 128
REL_BUCKETS = 32
REL_MAX_DIST = 128
N_GROUPS = 8
EXPERTS_PER_GROUP = 8
N_MOD = 6

VMEM_LIMIT_BYTES = 56 * 1024 * 1024
LANES = 128

NT_DIMS = (((1,), (1,)), ((), ()))


def _params(*sem):
    return pltpu.CompilerParams(dimension_semantics=sem, vmem_limit_bytes=VMEM_LIMIT_BYTES)


def _sigmoid(v):
    return 1.0 / (1.0 + jnp.exp(-v))


def _mod_kernel(c_ref, w_ref, b_ref, o_ref):
    c = c_ref[...]
    s = (c * _sigmoid(c)).astype(BF16)
    o_ref[...] = jnp.dot(s, w_ref[...].astype(BF16), preferred_element_type=F32) + b_ref[...]


def _adaln_mod(c, w_ada, b_ada):
    bsz, d = c.shape
    ncol = w_ada.shape[-1]
    tn = 1024
    return pl.pallas_call(
        _mod_kernel,
        out_shape=jax.ShapeDtypeStruct((bsz, ncol), F32),
        grid=(ncol // tn,),
        in_specs=[pl.BlockSpec((bsz, d), lambda j: (0, 0)),
                  pl.BlockSpec((None, d, tn), lambda j: (0, 0, j)),
                  pl.BlockSpec((1, tn), lambda j: (0, j))],
        out_specs=pl.BlockSpec((bsz, tn), lambda j: (0, j)),
        compiler_params=_params("arbitrary"),
        name="adaln_mod",
    )(c, w_ada, b_ada.reshape(1, ncol))


def _hnorm_kernel(x_ref, g_ref, sc_ref, sh_ref, o_ref):
    x = x_ref[...]
    ms = jnp.mean(x * x, axis=-1, keepdims=True)
    y = x * lax.rsqrt(ms + EPS) * g_ref[...]
    o_ref[...] = (y * (1.0 + sc_ref[...]) + sh_ref[...]).astype(o_ref.dtype)


def _norm_modulate(x, g, mod3, scale_idx, shift_idx):
    bsz, s, d = x.shape
    ts = 512
    return pl.pallas_call(
        _hnorm_kernel,
        out_shape=jax.ShapeDtypeStruct((bsz, s, d), BF16),
        grid=(bsz, s // ts),
        in_specs=[pl.BlockSpec((None, ts, d), lambda b, i: (b, i, 0)),
                  pl.BlockSpec((1, d), lambda b, i: (0, 0)),
                  pl.BlockSpec((None, 1, d), lambda b, i: (b, 0, scale_idx)),
                  pl.BlockSpec((None, 1, d), lambda b, i: (b, 0, shift_idx))],
        out_specs=pl.BlockSpec((None, ts, d), lambda b, i: (b, i, 0)),
        compiler_params=_params("arbitrary", "arbitrary"),
        name="norm_modulate",
    )(x, g.reshape(1, d), mod3, mod3)


def _proj_kernel(h_ref, w_ref, o_ref, wb_ref, *, gate):
    @pl.when(pl.program_id(1) == 0)
    def _():
        wb_ref[...] = w_ref[...].astype(BF16)

    r = jnp.dot(h_ref[...], wb_ref[...], preferred_element_type=F32)
    if gate:
        r = _sigmoid(r)
    o_ref[...] = r.astype(o_ref.dtype)


def _in_proj(h2d, w_in, col0, ncols, out_dtype, gate):
    n, d = h2d.shape
    tn, tm = 1024, 1024
    jb = col0 // tn
    return pl.pallas_call(
        functools.partial(_proj_kernel, gate=gate),
        out_shape=jax.ShapeDtypeStruct((n, ncols), out_dtype),
        grid=(ncols // tn, n // tm),
        in_specs=[pl.BlockSpec((tm, d), lambda j, i: (i, 0)),
                  pl.BlockSpec((None, d, tn), lambda j, i: (0, 0, j + jb))],
        out_specs=pl.BlockSpec((tm, tn), lambda j, i: (i, j)),
        scratch_shapes=[pltpu.VMEM((d, tn), BF16)],
        compiler_params=_params("arbitrary", "arbitrary"),
        name="in_proj_gate" if gate else "in_proj_qkv",
    )(h2d, w_in)


def _rel_bucket(n):
    n = jnp.maximum(n, 0)
    max_exact = REL_BUCKETS // 2
    nf = jnp.maximum(n, 1).astype(F32)
    large = max_exact + (jnp.log(nf / max_exact) / math.log(REL_MAX_DIST / max_exact)
                         * (REL_BUCKETS - max_exact)).astype(I32)
    large = jnp.minimum(large, REL_BUCKETS - 1)
    return jnp.where(n < max_exact, n, large)


def _bias_tiles(rel_table, t):
    assert 2 * t - (t - 1) >= REL_MAX_DIST, "far tiles must sit wholly in the last bucket"
    nb, h = rel_table.shape
    dist = jnp.arange(-(t - 1), 3 * t, dtype=I32)
    onehot = _rel_bucket(dist)[:, None] == jnp.arange(nb, dtype=I32)[None, :]
    by_dist = jnp.sum(jnp.where(onehot[:, :, None], rel_table[None].astype(F32), 0.0), axis=1).T
    r = jnp.arange(t, dtype=I32)[:, None]
    c = jnp.arange(t, dtype=I32)[None, :]
    tiles = []
    for delta in range(3):
        u = by_dist[:, delta * t:delta * t + 2 * t - 1][:, ::-1]
        rows = jnp.tile(jnp.pad(u, ((0, 0), (0, 1))), (1, t))[:, :t * (2 * t - 1)].reshape(h, t, 2 * t - 1)
        b = rows[:, :, t - 1:]
        if delta == 0:
            b = jnp.where((c <= r)[None], b, -jnp.inf)
        tiles.append(b)
    return jnp.stack(tiles, axis=1)


def _lane_chunks(x):
    return [x[:, c * LANES:(c + 1) * LANES] for c in range(x.shape[1] // LANES)]


def _diff_attn_kernel(lam_ref, g_ref, bias_ref, q_ref, k_ref, v_ref, o_ref, s1_ref, s2_ref,
                      *, t, nq, lam_init):
    qi = pl.program_id(2)
    lp = lam_ref[...]
    lam = (jnp.exp(jnp.sum(lp[0:1] * lp[1:2], axis=-1, keepdims=True))
           - jnp.exp(jnp.sum(lp[2:3] * lp[3:4], axis=-1, keepdims=True)) + lam_init)

    q = q_ref[...] * jnp.asarray(DA_HALF_DIM ** -0.5, BF16)
    lane = lax.broadcasted_iota(I32, q.shape, 1)
    q1 = jnp.where(lane < DA_HALF_DIM, q, jnp.zeros_like(q))
    q2 = jnp.where(lane >= DA_HALF_DIM, q, jnp.zeros_like(q))

    def scores(qh, s_ref, nk):
        mx = None
        for j in range(nk):
            s = (lax.dot_general(qh, k_ref[j * t:(j + 1) * t, :], NT_DIMS, preferred_element_type=F32)
                 + bias_ref[min(nk - 1 - j, 2)])
            s_ref[:, j * t:(j + 1) * t] = s
            for ch in _lane_chunks(s):
                mx = ch if mx is None else jnp.maximum(mx, ch)
        return jnp.broadcast_to(jnp.max(mx, axis=-1, keepdims=True), (t, LANES))

    def softmax_pv(s_ref, mb, nk):
        lsum = jnp.zeros((t, LANES), F32)
        acc = jnp.zeros((t, HEAD_DIM), F32)
        for j in range(nk):
            ps = [jnp.exp(ch - mb) for ch in _lane_chunks(s_ref[:, j * t:(j + 1) * t])]
            for p in ps:
                lsum = lsum + p
            acc = acc + jnp.dot(jnp.concatenate(ps, axis=1).astype(BF16), v_ref[j * t:(j + 1) * t, :],
                                preferred_element_type=F32)
        return acc / jnp.sum(lsum, axis=-1, keepdims=True)

    def variant(nk):
        m1 = scores(q1, s1_ref, nk)
        m2 = scores(q2, s2_ref, nk)
        o = softmax_pv(s1_ref, m1, nk) - lam * softmax_pv(s2_ref, m2, nk)
        ms = jnp.mean(o * o, axis=-1, keepdims=True)
        o = (o * lax.rsqrt(ms + EPS) * g_ref[...]) * (1.0 - lam_init)
        o_ref[...] = o.astype(o_ref.dtype)

    for i in range(nq):
        pl.when(qi == i)(functools.partial(variant, i + 1))


def _diff_attention(qkv3, lam_params, g_subln, bias, lam_init, t):
    bsz, s, _ = qkv3.shape
    h = DA_HEADS
    return pl.pallas_call(
        functools.partial(_diff_attn_kernel, t=t, nq=s // t, lam_init=lam_init),
        out_shape=jax.ShapeDtypeStruct((bsz, s, h * HEAD_DIM), BF16),
        scratch_shapes=[pltpu.VMEM((t, s), F32), pltpu.VMEM((t, s), F32)],
        grid=(bsz, h, s // t),
        in_specs=[pl.BlockSpec((4, DA_HALF_DIM), lambda b, hh, i: (0, 0)),
                  pl.BlockSpec((1, HEAD_DIM), lambda b, hh, i: (0, 0)),
                  pl.BlockSpec((None, 3, t, t), lambda b, hh, i: (hh, 0, 0, 0)),
                  pl.BlockSpec((None, t, HEAD_DIM), lambda b, hh, i: (b, i, hh)),
                  pl.BlockSpec((None, s, HEAD_DIM), lambda b, hh, i: (b, 0, h + hh)),
                  pl.BlockSpec((None, s, HEAD_DIM), lambda b, hh, i: (b, 0, 2 * h + hh))],
        out_specs=pl.BlockSpec((None, t, HEAD_DIM), lambda b, hh, i: (b, i, hh)),
        compiler_params=_params("arbitrary", "arbitrary", "arbitrary"),
        name="diff_attention",
    )(lam_params, g_subln.reshape(1, HEAD_DIM), bias, qkv3, qkv3, qkv3)


def _sb_attn_kernel(q_ref, k_ref, v_ref, o_ref, e_ref, *, t, nq, scale):
    qi = pl.program_id(2)
    q = q_ref[...]
    row = lax.broadcasted_iota(I32, (t, t), 0)
    col = lax.broadcasted_iota(I32, (t, t), 1)
    tri = (row >= col).astype(BF16)
    past = col < row

    def variant(nk):
        totals = []
        for j in range(nk):
            z = lax.dot_general(q, k_ref[j * t:(j + 1) * t, :], NT_DIMS, preferred_element_type=F32) * scale
            log_beta = jnp.minimum(z, 0.0) - jnp.log(1.0 + jnp.exp(-jnp.abs(z)))
            log_keep = log_beta - z
            if j == nk - 1:
                log_keep = jnp.where(past, log_keep, 0.0)
            hi = log_keep.astype(BF16)
            lo = (log_keep - hi.astype(F32)).astype(BF16)
            incl = (jnp.dot(hi, tri, preferred_element_type=F32)
                    + jnp.dot(lo, tri, preferred_element_type=F32))
            e_ref[:, j * t:(j + 1) * t] = z + incl
            totals.append(incl[:, 0:1])

        acc = jnp.zeros((t, HEAD_DIM), F32)
        later = jnp.zeros((t, 1), F32)
        for j in reversed(range(nk)):
            lb = jnp.broadcast_to(later, (t, LANES))
            ws = [jnp.exp(ch + lb) for ch in _lane_chunks(e_ref[:, j * t:(j + 1) * t])]
            w = jnp.concatenate(ws, axis=1)
            if j == nk - 1:
                w = jnp.where(past, w, 0.0)
            acc = acc + jnp.dot(w.astype(BF16), v_ref[j * t:(j + 1) * t, :], preferred_element_type=F32)
            later = later + totals[j]
        o_ref[...] = acc.astype(o_ref.dtype)

    for i in range(nq):
        pl.when(qi == i)(functools.partial(variant, i + 1))


def _sb_attention(qkv3, col_block0, t):
    bsz, s, _ = qkv3.shape
    h = SB_HEADS
    return pl.pallas_call(
        functools.partial(_sb_attn_kernel, t=t, nq=s // t, scale=HEAD_DIM ** -0.5),
        out_shape=jax.ShapeDtypeStruct((bsz, s, h * HEAD_DIM), BF16),
        scratch_shapes=[pltpu.VMEM((t, s), F32)],
        grid=(bsz, h, s // t),
        in_specs=[pl.BlockSpec((None, t, HEAD_DIM), lambda b, hh, i: (b, i, col_block0 + hh)),
                  pl.BlockSpec((None, s, HEAD_DIM), lambda b, hh, i: (b, 0, col_block0 + h + hh)),
                  pl.BlockSpec((None, s, HEAD_DIM), lambda b, hh, i: (b, 0, col_block0 + 2 * h + hh))],
        out_specs=pl.BlockSpec((None, t, HEAD_DIM), lambda b, hh, i: (b, i, hh)),
        compiler_params=_params("arbitrary", "arbitrary", "arbitrary"),
        name="sb_attention",
    )(qkv3, qkv3, qkv3)


def _post_kernel(oa_ref, ob_ref, sa_ref, sb_ref, x_ref, gm_ref, scf_ref, shf_ref, gffn_ref,
                 wa_ref, wb_ref, wo_ref, wrh_ref, wrl_ref, br_ref,
                 x1_ref, h2_ref, lg_ref):
    merged = (sa_ref[...] * jnp.dot(oa_ref[...], wa_ref[...], preferred_element_type=F32)
              + sb_ref[...] * jnp.dot(ob_ref[...], wb_ref[...], preferred_element_type=F32))
    y = jnp.dot(merged.astype(BF16), wo_ref[...], preferred_element_type=F32)
    x1 = x_ref[...] + gm_ref[...] * y
    x1_ref[...] = x1
    ms = jnp.mean(x1 * x1, axis=-1, keepdims=True)
    h2 = (x1 * lax.rsqrt(ms + EPS) * gffn_ref[...]) * (1.0 + scf_ref[...]) + shf_ref[...]
    h2_ref[...] = h2
    hb = h2.astype(BF16)
    hl = (h2 - hb.astype(F32)).astype(BF16)
    wrh = wrh_ref[...]
    lg = (lax.dot_general(wrh, hb, NT_DIMS, preferred_element_type=F32)
          + lax.dot_general(wrh, hl, NT_DIMS, preferred_element_type=F32)
          + lax.dot_general(wrl_ref[...], hb, NT_DIMS, preferred_element_type=F32))
    lg_ref[...] = lg + br_ref[...]


def _post_attention(oa, ob, gates, x2d, mod3, g_ffn, wa, wb, wo, wrh, wrl, br, seq):
    n, d = x2d.shape
    wa_rows = oa.shape[1]
    tm = 256
    per_b = seq // tm
    const = lambda shape: pl.BlockSpec(shape, lambda i: (0,) * len(shape), pipeline_mode=pl.Buffered(1))
    modspec = lambda idx: pl.BlockSpec((None, 1, d), lambda i: (i // per_b, 0, idx))
    return pl.pallas_call(
        _post_kernel,
        out_shape=(jax.ShapeDtypeStruct((n, d), F32),
                   jax.ShapeDtypeStruct((n, d), F32),
                   jax.ShapeDtypeStruct((LANES, n), F32)),
        grid=(n // tm,),
        in_specs=[pl.BlockSpec((tm, wa_rows), lambda i: (i, 0)),
                  pl.BlockSpec((tm, wa_rows), lambda i: (i, 0)),
                  pl.BlockSpec((tm, d), lambda i: (i, 0)),
                  pl.BlockSpec((tm, d), lambda i: (i, 1)),
                  pl.BlockSpec((tm, d), lambda i: (i, 0)),
                  modspec(2), modspec(4), modspec(3),
                  const((1, d)),
                  const((wa_rows, d)), const((wa_rows, d)), const((d, d)),
                  const((LANES, d)), const((LANES, d)), const((LANES, 1))],
        out_specs=(pl.BlockSpec((tm, d), lambda i: (i, 0)),
                   pl.BlockSpec((tm, d), lambda i: (i, 0)),
                   pl.BlockSpec((LANES, tm), lambda i: (0, i))),
        compiler_params=_params("arbitrary"),
        name="post_attention",
    )(oa, ob, gates, gates, x2d, mod3, mod3, mod3, g_ffn.reshape(1, d), wa, wb, wo, wrh, wrl, br)


def _first_index_of_max(vals, iota, nrows):
    mx = jnp.max(vals, axis=0, keepdims=True)
    idx = jnp.min(jnp.where(vals == mx, iota, nrows), axis=0, keepdims=True)
    return mx, idx


def _route_kernel(lg_ref, e_ref, w_ref):
    g = N_GROUPS
    epg = EXPERTS_PER_GROUP
    lg = lg_ref[...]
    gl = lg[0:g, :]
    iota = lax.broadcasted_iota(I32, gl.shape, 0)
    gmax, gidx = _first_index_of_max(gl, iota, g)
    p_g = 1.0 / jnp.sum(jnp.exp(gl - gmax), axis=0, keepdims=True)

    esel = jnp.zeros((epg, lg.shape[1]), F32)
    for gi in range(g):
        esel = jnp.where(gidx == gi, lg[g + gi * epg:g + (gi + 1) * epg, :], esel)
    emax = jnp.max(esel, axis=0, keepdims=True)
    ex = jnp.exp(esel - emax)
    prob = ex / jnp.sum(ex, axis=0, keepdims=True)

    p0, i0 = _first_index_of_max(prob, iota, epg)
    rest = jnp.where(iota == i0, -1.0, prob)
    p1, i1 = _first_index_of_max(rest, iota, epg)
    tot = p0 + p1
    e_ref[0:1, :] = gidx * epg + i0
    e_ref[1:2, :] = gidx * epg + i1
    w_ref[0:1, :] = p_g * (p0 / tot)
    w_ref[1:2, :] = p_g * (p1 / tot)


def _route(logits_t):
    rows, n = logits_t.shape
    tn = 1024
    return pl.pallas_call(
        _route_kernel,
        out_shape=(jax.ShapeDtypeStruct((2, n), I32), jax.ShapeDtypeStruct((2, n), F32)),
        grid=(n // tn,),
        in_specs=[pl.BlockSpec((rows, tn), lambda i: (0, i))],
        out_specs=(pl.BlockSpec((2, tn), lambda i: (0, i)), pl.BlockSpec((2, tn), lambda i: (0, i))),
        compiler_params=_params("arbitrary"),
        name="route",
    )(logits_t)


VISIT_ROWS = 512
MOE_TILE = 256
MOE_FCHUNK = 512


def _moe_kernel(ve_ref, vnt_ref,
                tokc_ref, tokn_ref, h_hbm, wg_ref, wu_ref, wd_ref,
                y_ref,
                xbuf, sem, wgb, wub, wdb, *, nf, nv):
    v = pl.program_id(0)
    f = pl.program_id(1)
    slot = v % 2
    nt = vnt_ref[v]
    part = VISIT_ROWS // nf

    def row_copy(tok_ref, dst_slot, r):
        tok = tok_ref[0, r]
        return pltpu.make_async_copy(h_hbm.at[pl.ds(tok, 1)], xbuf.at[dst_slot, pl.ds(r, 1)],
                                     sem.at[dst_slot])

    def issue(tok_ref, dst_slot, r0, nrows):
        def body(r, c):
            row_copy(tok_ref, dst_slot, r0 + r).start()
            return c
        lax.fori_loop(0, nrows, body, 0, unroll=8)

    @pl.when(jnp.logical_and(v == 0, f == 0))
    def _():
        for tl in range(VISIT_ROWS // MOE_TILE):
            @pl.when(tl < nt)
            def _():
                issue(tokc_ref, 0, tl * MOE_TILE, MOE_TILE)

    @pl.when(jnp.logical_and(f == 0, nt > 0))
    def _():
        for tl in range(VISIT_ROWS // MOE_TILE):
            @pl.when(tl < nt)
            def _():
                rows = pl.ds(tl * MOE_TILE, MOE_TILE)
                pltpu.make_async_copy(h_hbm.at[pl.ds(0, MOE_TILE)], xbuf.at[slot, rows],
                                      sem.at[slot]).wait()

    @pl.when(f == 0)
    def _():
        y_ref[...] = jnp.zeros_like(y_ref)

    nt_next = jnp.where(v + 1 < nv, vnt_ref[jnp.minimum(v + 1, nv - 1)], 0)

    piece = min(part, MOE_TILE)
    for sub in range(part // piece):
        r0 = f * part + sub * piece

        @pl.when(nt_next * MOE_TILE > r0)
        def _():
            issue(tokn_ref, 1 - slot, r0, piece)

    @pl.when(nt > 0)
    def _():
        wgb[...] = wg_ref[...].astype(BF16)
        wub[...] = wu_ref[...].astype(BF16)
        wdb[...] = wd_ref[...].astype(BF16)
        for tl in range(VISIT_ROWS // MOE_TILE):
            @pl.when(tl < nt)
            def _():
                rows = pl.ds(tl * MOE_TILE, MOE_TILE)
                x = xbuf[slot, rows, :].astype(BF16)
                a = jnp.dot(x, wgb[...], preferred_element_type=F32)
                u = jnp.dot(x, wub[...], preferred_element_type=F32)
                hmid = ((a * _sigmoid(a)) * u).astype(BF16)
                y_ref[rows, :] += jnp.dot(hmid, wdb[...], preferred_element_type=F32)


def _moe_experts(h2, row_tok, vis_e, vis_nt, w_gate, w_up, w_down):
    n, d = h2.shape
    n_exp, _, dexp = w_gate.shape
    nv = vis_e.shape[0]
    nf = dexp // MOE_FCHUNK
    tok3 = row_tok.reshape(nv, 1, VISIT_ROWS)

    def fidx(v, f, vnt):
        return jnp.where(vnt[v] > 0, f, nf - 1)

    grid_spec = pltpu.PrefetchScalarGridSpec(
        num_scalar_prefetch=2,
        grid=(nv, nf),
        in_specs=[
            pl.BlockSpec((None, 1, VISIT_ROWS), lambda v, f, ve, vnt: (v, 0, 0),
                         memory_space=pltpu.SMEM),
            pl.BlockSpec((None, 1, VISIT_ROWS), lambda v, f, ve, vnt: (jnp.minimum(v + 1, nv - 1), 0, 0),
                         memory_space=pltpu.SMEM),
            pl.BlockSpec(memory_space=pl.ANY),
            pl.BlockSpec((None, d, MOE_FCHUNK), lambda v, f, ve, vnt: (ve[v], 0, fidx(v, f, vnt))),
            pl.BlockSpec((None, d, MOE_FCHUNK), lambda v, f, ve, vnt: (ve[v], 0, fidx(v, f, vnt))),
            pl.BlockSpec((None, MOE_FCHUNK, d), lambda v, f, ve, vnt: (ve[v], fidx(v, f, vnt), 0)),
        ],
        out_specs=pl.BlockSpec((VISIT_ROWS, d), lambda v, f, ve, vnt: (v, 0)),
        scratch_shapes=[pltpu.VMEM((2, VISIT_ROWS, d), F32),
                        pltpu.SemaphoreType.DMA((2,)),
                        pltpu.VMEM((d, MOE_FCHUNK), BF16),
                        pltpu.VMEM((d, MOE_FCHUNK), BF16),
                        pltpu.VMEM((MOE_FCHUNK, d), BF16)],
    )
    return pl.pallas_call(
        functools.partial(_moe_kernel, nf=nf, nv=nv),
        out_shape=jax.ShapeDtypeStruct((nv * VISIT_ROWS, d), F32),
        grid_spec=grid_spec,
        compiler_params=_params("arbitrary", "arbitrary"),
        name="moe_experts",
    )(vis_e, vis_nt, tok3, tok3, h2, w_gate, w_up, w_down)


def _expert_layout(e2, n_exp):
    n = e2.shape[1]
    m = 2 * n
    nv = m // VISIT_ROWS + n_exp
    flat_e = e2.T.reshape(-1)
    flat_tok = jnp.repeat(jnp.arange(n, dtype=I32), 2)
    onehot = (flat_e[:, None] == jnp.arange(n_exp, dtype=I32)[None, :]).astype(I32)
    csum = jnp.cumsum(onehot, axis=0)
    rank = jnp.take_along_axis(csum, flat_e[:, None], axis=1)[:, 0] - 1
    counts = csum[-1]
    n_vis = (counts + VISIT_ROWS - 1) // VISIT_ROWS
    cum_vis = jnp.cumsum(n_vis)
    vbase = cum_vis - n_vis
    dest = vbase[flat_e] * VISIT_ROWS + rank
    row_tok = jnp.zeros((nv * VISIT_ROWS,), I32).at[dest].set(flat_tok)

    n_used = cum_vis[-1]
    vid = jnp.arange(nv, dtype=I32)
    used = vid < n_used
    ve = jnp.minimum(jnp.searchsorted(cum_vis, vid, side="right").astype(I32), n_exp - 1)
    rem = counts[ve] - (vid - vbase[ve]) * VISIT_ROWS
    nt = jnp.clip((rem + MOE_TILE - 1) // MOE_TILE, 0, VISIT_ROWS // MOE_TILE)
    last = jnp.maximum(n_used - 1, 0)
    vis_nt = jnp.where(used, nt, 0).astype(I32)
    vis_e = jnp.where(used, ve, ve[last]).astype(I32)
    return dest.reshape(n, 2), row_tok, vis_e, vis_nt


def _final_kernel(posc_ref, posn_ref, x1_ref, gf_ref, w_ref, g_ref, ys_hbm, o_ref, ybuf, sem,
                  *, tm, nsteps):
    i = pl.program_id(0)
    slot = i % 2

    def row_copy(pos_ref, dst_slot, j):
        return pltpu.make_async_copy(ys_hbm.at[pl.ds(pos_ref[0, j], 1)],
                                     ybuf.at[dst_slot, pl.ds(j, 1)], sem.at[dst_slot])

    def issue(pos_ref, dst_slot):
        def body(j, c):
            row_copy(pos_ref, dst_slot, j).start()
            return c
        lax.fori_loop(0, 2 * tm, body, 0, unroll=8)

    @pl.when(i == 0)
    def _():
        issue(posc_ref, 0)

    @pl.when(i + 1 < nsteps)
    def _():
        issue(posn_ref, 1 - slot)

    pltpu.make_async_copy(ys_hbm.at[pl.ds(0, 2 * tm)], ybuf.at[slot], sem.at[slot]).wait()

    w = w_ref[...]
    moe = w[:, 0:1] * ybuf[slot, 0:tm, :] + w[:, 1:2] * ybuf[slot, tm:2 * tm, :]
    x = x1_ref[...] + gf_ref[...] * moe
    ms = jnp.mean(x * x, axis=-1, keepdims=True)
    o_ref[...] = x * lax.rsqrt(ms + EPS) * g_ref[...]


def _combine_final(x1, ys, pos, gate_w, mod3, g_final, seq):
    n, d = x1.shape
    tm = 256
    nsteps = n // tm
    per_b = seq // tm
    pos3 = pos.reshape(nsteps, tm, 2).transpose(0, 2, 1).reshape(nsteps, 1, 2 * tm)
    return pl.pallas_call(
        functools.partial(_final_kernel, tm=tm, nsteps=nsteps),
        out_shape=jax.ShapeDtypeStruct((n, d), F32),
        grid=(nsteps,),
        in_specs=[pl.BlockSpec((None, 1, 2 * tm), lambda i: (i, 0, 0), memory_space=pltpu.SMEM),
                  pl.BlockSpec((None, 1, 2 * tm), lambda i: (jnp.minimum(i + 1, nsteps - 1), 0, 0),
                               memory_space=pltpu.SMEM),
                  pl.BlockSpec((tm, d), lambda i: (i, 0)),
                  pl.BlockSpec((None, 1, d), lambda i: (i // per_b, 0, 5)),
                  pl.BlockSpec((tm, 2), lambda i: (i, 0)),
                  pl.BlockSpec((1, d), lambda i: (0, 0)),
                  pl.BlockSpec(memory_space=pl.ANY)],
        out_specs=pl.BlockSpec((tm, d), lambda i: (i, 0)),
        scratch_shapes=[pltpu.VMEM((2, 2 * tm, d), F32), pltpu.SemaphoreType.DMA((2,))],
        compiler_params=_params("arbitrary"),
        name="combine_final",
    )(pos3, pos3, x1, mod3, gate_w, g_final.reshape(1, d), ys)


def kernel(x, c, rel_bias_table, w_ada, b_ada, g_mix, w_in, lambda_q1, lambda_k1, lambda_q2,
           lambda_k2, g_subln, w_proj_a, w_proj_b, w_out, g_ffn, w_router_group, b_router_group,
           w_router_expert, b_router_expert, w_expert_gate, w_expert_up, w_expert_down, g_final):
    bsz, seq, d = x.shape
    n = bsz * seq
    depth = w_in.shape[0]
    assert depth == 1, "the MoE combine is fused with the final RMSNorm: one layer only"
    da_width = DA_HEADS * HEAD_DIM
    sb_width = SB_HEADS * HEAD_DIM
    qkv_cols = 3 * da_width + 3 * sb_width
    attn_tile = 256
    n_exp = w_expert_gate.shape[1]
    xf = x.reshape(n, d)

    for l in range(depth):
        lam_init = 0.8 - 0.6 * math.exp(-0.3 * l)
        mod = _adaln_mod(c, w_ada[l:l + 1], b_ada[l])
        mod3 = mod.reshape(bsz, 1, N_MOD * d)

        h = _norm_modulate(xf.reshape(bsz, seq, d), g_mix[l], mod3, 1, 0).reshape(n, d)
        qkv = _in_proj(h, w_in[l:l + 1], 0, qkv_cols, BF16, gate=False)
        gates = _in_proj(h, w_in[l:l + 1], qkv_cols, 2 * d, F32, gate=True)
        qkv3 = qkv.reshape(bsz, seq, qkv_cols)

        lam_params = jnp.stack([lambda_q1[l], lambda_k1[l], lambda_q2[l], lambda_k2[l]]).astype(F32)
        bias = _bias_tiles(rel_bias_table, attn_tile)
        oa = _diff_attention(qkv3, lam_params, g_subln[l], bias, lam_init, attn_tile)
        ob = _sb_attention(qkv3, 3 * DA_HEADS, attn_tile)

        w_r = jnp.concatenate([w_router_group[l], w_router_expert[l]], axis=1).astype(F32).T
        w_r = jnp.pad(w_r, ((0, LANES - w_r.shape[0]), (0, 0)))
        wrh = w_r.astype(BF16)
        wrl = (w_r - wrh.astype(F32)).astype(BF16)
        b_r = jnp.concatenate([b_router_group[l], b_router_expert[l]]).astype(F32)
        b_r = jnp.pad(b_r, (0, LANES - b_r.shape[0])).reshape(LANES, 1)
        x1, h2, logits_t = _post_attention(
            oa.reshape(n, da_width), ob.reshape(n, sb_width), gates, xf, mod3, g_ffn[l],
            w_proj_a[l].astype(BF16), w_proj_b[l].astype(BF16), w_out[l].astype(BF16),
            wrh, wrl, b_r, seq)

        e2, g2 = _route(logits_t)
        pos, row_tok, vis_e, vis_nt = _expert_layout(e2, n_exp)
        ys = _moe_experts(h2, row_tok, vis_e, vis_nt,
                          w_expert_gate[l], w_expert_up[l], w_expert_down[l])
        out = _combine_final(x1, ys, pos, g2.T, mod3, g_final, seq)
    return out.reshape(bsz, seq, d)
```

```python
import functools
import math

import jax
import jax.numpy as jnp
from jax import lax
from jax.experimental import pallas as pl
from jax.experimental.pallas import tpu as pltpu

F32 = jnp.float32
BF16 = jnp.bfloat16
I32 = jnp.int32
EPS = 1e-6

DA_HEADS = 8
DA_HALF_DIM = 64
SB_HEADS = 8
HEAD_DIM = 128
REL_BUCKETS = 32
REL_MAX_DIST = 128
N_GROUPS = 8
EXPERTS_PER_GROUP = 8
N_MOD = 6

VMEM_LIMIT_BYTES = 56 * 1024 * 1024
LANES = 128

NT_DIMS = (((1,), (1,)), ((), ()))


def _params(*sem):
    return pltpu.CompilerParams(dimension_semantics=sem, vmem_limit_bytes=VMEM_LIMIT_BYTES)


def _sigmoid(v):
    return 1.0 / (1.0 + jnp.exp(-v))


def _mod_kernel(c_ref, w_ref, b_ref, o_ref):
    c = c_ref[...]
    s = (c * _sigmoid(c)).astype(BF16)
    o_ref[...] = jnp.dot(s, w_ref[...].astype(BF16), preferred_element_type=F32) + b_ref[...]


def _adaln_mod(c, w_ada, b_ada):
    bsz, d = c.shape
    ncol = w_ada.shape[-1]
    tn = 1024
    return pl.pallas_call(
        _mod_kernel,
        out_shape=jax.ShapeDtypeStruct((bsz, ncol), F32),
        grid=(ncol // tn,),
        in_specs=[pl.BlockSpec((bsz, d), lambda j: (0, 0)),
                  pl.BlockSpec((None, d, tn), lambda j: (0, 0, j)),
                  pl.BlockSpec((1, tn), lambda j: (0, j))],
        out_specs=pl.BlockSpec((bsz, tn), lambda j: (0, j)),
        compiler_params=_params("arbitrary"),
        name="adaln_mod",
    )(c, w_ada, b_ada.reshape(1, ncol))


def _hnorm_kernel(x_ref, g_ref, sc_ref, sh_ref, o_ref):
    x = x_ref[...]
    ms = jnp.mean(x * x, axis=-1, keepdims=True)
    y = x * lax.rsqrt(ms + EPS) * g_ref[...]
    o_ref[...] = (y * (1.0 + sc_ref[...]) + sh_ref[...]).astype(o_ref.dtype)


def _norm_modulate(x, g, mod3, scale_idx, shift_idx):
    bsz, s, d = x.shape
    ts = 512
    return pl.pallas_call(
        _hnorm_kernel,
        out_shape=jax.ShapeDtypeStruct((bsz, s, d), BF16),
        grid=(bsz, s // ts),
        in_specs=[pl.BlockSpec((None, ts, d), lambda b, i: (b, i, 0)),
                  pl.BlockSpec((1, d), lambda b, i: (0, 0)),
                  pl.BlockSpec((None, 1, d), lambda b, i: (b, 0, scale_idx)),
                  pl.BlockSpec((None, 1, d), lambda b, i: (b, 0, shift_idx))],
        out_specs=pl.BlockSpec((None, ts, d), lambda b, i: (b, i, 0)),
        compiler_params=_params("arbitrary", "arbitrary"),
        name="norm_modulate",
    )(x, g.reshape(1, d), mod3, mod3)


def _proj_kernel(h_ref, w_ref, o_ref, wb_ref, *, gate):
    @pl.when(pl.program_id(1) == 0)
    def _():
        wb_ref[...] = w_ref[...].astype(BF16)

    r = jnp.dot(h_ref[...], wb_ref[...], preferred_element_type=F32)
    if gate:
        r = _sigmoid(r)
    o_ref[...] = r.astype(o_ref.dtype)


def _in_proj(h2d, w_in, col0, ncols, out_dtype, gate):
    n, d = h2d.shape
    tn, tm = 1024, 1024
    jb = col0 // tn
    return pl.pallas_call(
        functools.partial(_proj_kernel, gate=gate),
        out_shape=jax.ShapeDtypeStruct((n, ncols), out_dtype),
        grid=(ncols // tn, n // tm),
        in_specs=[pl.BlockSpec((tm, d), lambda j, i: (i, 0)),
                  pl.BlockSpec((None, d, tn), lambda j, i: (0, 0, j + jb))],
        out_specs=pl.BlockSpec((tm, tn), lambda j, i: (i, j)),
        scratch_shapes=[pltpu.VMEM((d, tn), BF16)],
        compiler_params=_params("arbitrary", "arbitrary"),
        name="in_proj_gate" if gate else "in_proj_qkv",
    )(h2d, w_in)


def _rel_bucket(n):
    n = jnp.maximum(n, 0)
    max_exact = REL_BUCKETS // 2
    nf = jnp.maximum(n, 1).astype(F32)
    large = max_exact + (jnp.log(nf / max_exact) / math.log(REL_MAX_DIST / max_exact)
                         * (REL_BUCKETS - max_exact)).astype(I32)
    large = jnp.minimum(large, REL_BUCKETS - 1)
    return jnp.where(n < max_exact, n, large)


def _bias_tiles(rel_table, t):
    assert 2 * t - (t - 1) >= REL_MAX_DIST, "far tiles must sit wholly in the last bucket"
    nb, h = rel_table.shape
    dist = jnp.arange(-(t - 1), 3 * t, dtype=I32)
    onehot = _rel_bucket(dist)[:, None] == jnp.arange(nb, dtype=I32)[None, :]
    by_dist = jnp.sum(jnp.where(onehot[:, :, None], rel_table[None].astype(F32), 0.0), axis=1).T
    r = jnp.arange(t, dtype=I32)[:, None]
    c = jnp.arange(t, dtype=I32)[None, :]
    tiles = []
    for delta in range(3):
        u = by_dist[:, delta * t:delta * t + 2 * t - 1][:, ::-1]
        rows = jnp.tile(jnp.pad(u, ((0, 0), (0, 1))), (1, t))[:, :t * (2 * t - 1)].reshape(h, t, 2 * t - 1)
        b = rows[:, :, t - 1:]
        if delta == 0:
            b = jnp.where((c <= r)[None], b, -jnp.inf)
        tiles.append(b)
    return jnp.stack(tiles, axis=1)


def _lane_chunks(x):
    return [x[:, c * LANES:(c + 1) * LANES] for c in range(x.shape[1] // LANES)]


def _diff_attn_kernel(lam_ref, g_ref, bias_ref, q_ref, k_ref, v_ref, o_ref, s1_ref, s2_ref,
                      *, t, nsteps, qpt, lam_init):
    step = pl.program_id(2)
    lp = lam_ref[...]
    lam = (jnp.exp(jnp.sum(lp[0:1] * lp[1:2], axis=-1, keepdims=True))
           - jnp.exp(jnp.sum(lp[2:3] * lp[3:4], axis=-1, keepdims=True)) + lam_init)
    lane = lax.broadcasted_iota(I32, (t, HEAD_DIM), 1)

    def scores(qh, s_ref, nk):
        mx = None
        for j in range(nk):
            s = (lax.dot_general(qh, k_ref[j * t:(j + 1) * t, :], NT_DIMS, preferred_element_type=F32)
                 + bias_ref[min(nk - 1 - j, 2)])
            s_ref[:, j * t:(j + 1) * t] = s
            for ch in _lane_chunks(s):
                mx = ch if mx is None else jnp.maximum(mx, ch)
        return jnp.broadcast_to(jnp.max(mx, axis=-1, keepdims=True), (t, LANES))

    def softmax_pv(s_ref, mb, nk):
        lsum = jnp.zeros((t, LANES), F32)
        acc = jnp.zeros((t, HEAD_DIM), F32)
        for j in range(nk):
            ps = [jnp.exp(ch - mb) for ch in _lane_chunks(s_ref[:, j * t:(j + 1) * t])]
            for p in ps:
                lsum = lsum + p
            acc = acc + jnp.dot(jnp.concatenate(ps, axis=1).astype(BF16), v_ref[j * t:(j + 1) * t, :],
                                preferred_element_type=F32)
        return acc / jnp.sum(lsum, axis=-1, keepdims=True)

    def q_tile(lt, nk):
        rows = slice(lt * t, (lt + 1) * t)
        q = q_ref[rows, :] * jnp.asarray(DA_HALF_DIM ** -0.5, BF16)
        q1 = jnp.where(lane < DA_HALF_DIM, q, jnp.zeros_like(q))
        q2 = jnp.where(lane >= DA_HALF_DIM, q, jnp.zeros_like(q))
        s1, s2 = s1_ref.at[lt % 2], s2_ref.at[lt % 2]
        m1 = scores(q1, s1, nk)
        m2 = scores(q2, s2, nk)
        o = softmax_pv(s1, m1, nk) - lam * softmax_pv(s2, m2, nk)
        ms = jnp.mean(o * o, axis=-1, keepdims=True)
        o = (o * lax.rsqrt(ms + EPS) * g_ref[...]) * (1.0 - lam_init)
        o_ref[rows, :] = o.astype(o_ref.dtype)

    def variant(first_tile):
        for lt in range(qpt):
            q_tile(lt, first_tile + lt + 1)

    for i in range(nsteps):
        pl.when(step == i)(functools.partial(variant, i * qpt))


def _diff_attention(qkv3, lam_params, g_subln, bias, lam_init, t, qpt):
    bsz, s, _ = qkv3.shape
    h = DA_HEADS
    nsteps = s // (t * qpt)
    return pl.pallas_call(
        functools.partial(_diff_attn_kernel, t=t, nsteps=nsteps, qpt=qpt, lam_init=lam_init),
        out_shape=jax.ShapeDtypeStruct((bsz, s, h * HEAD_DIM), BF16),
        scratch_shapes=[pltpu.VMEM((2, t, s), F32), pltpu.VMEM((2, t, s), F32)],
        grid=(bsz, h, nsteps),
        in_specs=[pl.BlockSpec((4, DA_HALF_DIM), lambda b, hh, i: (0, 0)),
                  pl.BlockSpec((1, HEAD_DIM), lambda b, hh, i: (0, 0)),
                  pl.BlockSpec((None, 3, t, t), lambda b, hh, i: (hh, 0, 0, 0)),
                  pl.BlockSpec((None, t * qpt, HEAD_DIM), lambda b, hh, i: (b, i, hh)),
                  pl.BlockSpec((None, s, HEAD_DIM), lambda b, hh, i: (b, 0, h + hh)),
                  pl.BlockSpec((None, s, HEAD_DIM), lambda b, hh, i: (b, 0, 2 * h + hh))],
        out_specs=pl.BlockSpec((None, t * qpt, HEAD_DIM), lambda b, hh, i: (b, i, hh)),
        compiler_params=_params("arbitrary", "arbitrary", "arbitrary"),
        name="diff_attention",
    )(lam_params, g_subln.reshape(1, HEAD_DIM), bias, qkv3, qkv3, qkv3)


def _sb_attn_kernel(q_ref, k_ref, v_ref, o_ref, e_ref, *, t, nsteps, qpt, scale):
    step = pl.program_id(2)
    row = lax.broadcasted_iota(I32, (t, t), 0)
    col = lax.broadcasted_iota(I32, (t, t), 1)
    neg_tri = jnp.where(row >= col, -1.0, 0.0).astype(BF16)
    past = col < row

    def q_tile(lt, nk):
        rows = slice(lt * t, (lt + 1) * t)
        q = q_ref[rows, :]
        e = e_ref.at[lt % 2]
        totals = []
        for j in range(nk):
            z = lax.dot_general(q, k_ref[j * t:(j + 1) * t, :], NT_DIMS, preferred_element_type=F32) * scale
            softplus = jnp.maximum(z, 0.0) + jnp.log(1.0 + jnp.exp(-jnp.abs(z)))
            if j == nk - 1:
                softplus = jnp.where(past, softplus, 0.0)
            hi = softplus.astype(BF16)
            lo = (softplus - hi.astype(F32)).astype(BF16)
            incl = (jnp.dot(hi, neg_tri, preferred_element_type=F32)
                    + jnp.dot(lo, neg_tri, preferred_element_type=F32))
            e[:, j * t:(j + 1) * t] = z + incl
            totals.append(incl[:, 0:1])

        acc = jnp.zeros((t, HEAD_DIM), F32)
        later = jnp.zeros((t, 1), F32)
        for j in reversed(range(nk)):
            lb = jnp.broadcast_to(later, (t, LANES))
            ws = [jnp.exp(ch + lb) for ch in _lane_chunks(e[:, j * t:(j + 1) * t])]
            w = jnp.concatenate(ws, axis=1)
            if j == nk - 1:
                w = jnp.where(past, w, 0.0)
            acc = acc + jnp.dot(w.astype(BF16), v_ref[j * t:(j + 1) * t, :], preferred_element_type=F32)
            later = later + totals[j]
        o_ref[rows, :] = acc.astype(o_ref.dtype)

    def variant(first_tile):
        for lt in range(qpt):
            q_tile(lt, first_tile + lt + 1)

    for i in range(nsteps):
        pl.when(step == i)(functools.partial(variant, i * qpt))


def _sb_attention(qkv3, col_block0, t, qpt):
    bsz, s, _ = qkv3.shape
    h = SB_HEADS
    nsteps = s // (t * qpt)
    return pl.pallas_call(
        functools.partial(_sb_attn_kernel, t=t, nsteps=nsteps, qpt=qpt, scale=HEAD_DIM ** -0.5),
        out_shape=jax.ShapeDtypeStruct((bsz, s, h * HEAD_DIM), BF16),
        scratch_shapes=[pltpu.VMEM((2, t, s), F32)],
        grid=(bsz, h, nsteps),
        in_specs=[pl.BlockSpec((None, t * qpt, HEAD_DIM), lambda b, hh, i: (b, i, col_block0 + hh)),
                  pl.BlockSpec((None, s, HEAD_DIM), lambda b, hh, i: (b, 0, col_block0 + h + hh)),
                  pl.BlockSpec((None, s, HEAD_DIM), lambda b, hh, i: (b, 0, col_block0 + 2 * h + hh))],
        out_specs=pl.BlockSpec((None, t * qpt, HEAD_DIM), lambda b, hh, i: (b, i, hh)),
        compiler_params=_params("arbitrary", "arbitrary", "arbitrary"),
        name="sb_attention",
    )(qkv3, qkv3, qkv3)


def _post_kernel(oa_ref, ob_ref, sa_ref, sb_ref, x_ref, gm_ref, scf_ref, shf_ref, gffn_ref,
                 wa_ref, wb_ref, wo_ref, wrh_ref, wrl_ref, br_ref,
                 x1_ref, h2_ref, lg_ref):
    merged = (sa_ref[...] * jnp.dot(oa_ref[...], wa_ref[...], preferred_element_type=F32)
              + sb_ref[...] * jnp.dot(ob_ref[...], wb_ref[...], preferred_element_type=F32))
    y = jnp.dot(merged.astype(BF16), wo_ref[...], preferred_element_type=F32)
    x1 = x_ref[...] + gm_ref[...] * y
    x1_ref[...] = x1
    ms = jnp.mean(x1 * x1, axis=-1, keepdims=True)
    h2 = (x1 * lax.rsqrt(ms + EPS) * gffn_ref[...]) * (1.0 + scf_ref[...]) + shf_ref[...]
    h2_ref[...] = h2
    hb = h2.astype(BF16)
    hl = (h2 - hb.astype(F32)).astype(BF16)
    wrh = wrh_ref[...]
    lg = (lax.dot_general(wrh, hb, NT_DIMS, preferred_element_type=F32)
          + lax.dot_general(wrh, hl, NT_DIMS, preferred_element_type=F32)
          + lax.dot_general(wrl_ref[...], hb, NT_DIMS, preferred_element_type=F32))
    lg_ref[...] = lg + br_ref[...]


def _post_attention(oa, ob, gates, x2d, mod3, g_ffn, wa, wb, wo, wrh, wrl, br, seq):
    n, d = x2d.shape
    wa_rows = oa.shape[1]
    tm = 256
    per_b = seq // tm
    const = lambda shape: pl.BlockSpec(shape, lambda i: (0,) * len(shape), pipeline_mode=pl.Buffered(1))
    modspec = lambda idx: pl.BlockSpec((None, 1, d), lambda i: (i // per_b, 0, idx))
    return pl.pallas_call(
        _post_kernel,
        out_shape=(jax.ShapeDtypeStruct((n, d), F32),
                   jax.ShapeDtypeStruct((n, d), F32),
                   jax.ShapeDtypeStruct((LANES, n), F32)),
        grid=(n // tm,),
        in_specs=[pl.BlockSpec((tm, wa_rows), lambda i: (i, 0)),
                  pl.BlockSpec((tm, wa_rows), lambda i: (i, 0)),
                  pl.BlockSpec((tm, d), lambda i: (i, 0)),
                  pl.BlockSpec((tm, d), lambda i: (i, 1)),
                  pl.BlockSpec((tm, d), lambda i: (i, 0)),
                  modspec(2), modspec(4), modspec(3),
                  const((1, d)),
                  const((wa_rows, d)), const((wa_rows, d)), const((d, d)),
                  const((LANES, d)), const((LANES, d)), const((LANES, 1))],
        out_specs=(pl.BlockSpec((tm, d), lambda i: (i, 0)),
                   pl.BlockSpec((tm, d), lambda i: (i, 0)),
                   pl.BlockSpec((LANES, tm), lambda i: (0, i))),
        compiler_params=_params("arbitrary"),
        name="post_attention",
    )(oa, ob, gates, gates, x2d, mod3, mod3, mod3, g_ffn.reshape(1, d), wa, wb, wo, wrh, wrl, br)


def _first_index_of_max(vals, iota, nrows):
    mx = jnp.max(vals, axis=0, keepdims=True)
    idx = jnp.min(jnp.where(vals == mx, iota, nrows), axis=0, keepdims=True)
    return mx, idx


def _route_kernel(lg_ref, e_ref, w_ref):
    g = N_GROUPS
    epg = EXPERTS_PER_GROUP
    lg = lg_ref[...]
    gl = lg[0:g, :]
    iota = lax.broadcasted_iota(I32, gl.shape, 0)
    gmax, gidx = _first_index_of_max(gl, iota, g)
    p_g = 1.0 / jnp.sum(jnp.exp(gl - gmax), axis=0, keepdims=True)

    esel = jnp.zeros((epg, lg.shape[1]), F32)
    for gi in range(g):
        esel = jnp.where(gidx == gi, lg[g + gi * epg:g + (gi + 1) * epg, :], esel)
    emax = jnp.max(esel, axis=0, keepdims=True)
    ex = jnp.exp(esel - emax)
    prob = ex / jnp.sum(ex, axis=0, keepdims=True)

    p0, i0 = _first_index_of_max(prob, iota, epg)
    rest = jnp.where(iota == i0, -1.0, prob)
    p1, i1 = _first_index_of_max(rest, iota, epg)
    tot = p0 + p1
    e_ref[0:1, :] = gidx * epg + i0
    e_ref[1:2, :] = gidx * epg + i1
    w_ref[0:1, :] = p_g * (p0 / tot)
    w_ref[1:2, :] = p_g * (p1 / tot)


def _route(logits_t):
    rows, n = logits_t.shape
    tn = 1024
    return pl.pallas_call(
        _route_kernel,
        out_shape=(jax.ShapeDtypeStruct((2, n), I32), jax.ShapeDtypeStruct((2, n), F32)),
        grid=(n // tn,),
        in_specs=[pl.BlockSpec((rows, tn), lambda i: (0, i))],
        out_specs=(pl.BlockSpec((2, tn), lambda i: (0, i)), pl.BlockSpec((2, tn), lambda i: (0, i))),
        compiler_params=_params("arbitrary"),
        name="route",
    )(logits_t)


VISIT_ROWS = 512
MOE_TILE = 256
MOE_FCHUNK = 512
GATHER_CHUNK = 32


def _moe_kernel(ve_ref, vnt_ref, vcnt_ref,
                tokc_ref, tokn_ref, h_hbm, wg_ref, wu_ref, wd_ref,
                y_ref,
                xbuf, sem, wgb, wub, wdb, *, nf, nv):
    v = pl.program_id(0)
    f = pl.program_id(1)
    slot = v % 2
    nt = vnt_ref[v]
    chunks_per_step = VISIT_ROWS // nf // GATHER_CHUNK

    def n_chunks(vv):
        return lax.shift_right_logical(vcnt_ref[vv] + (GATHER_CHUNK - 1), GATHER_CHUNK.bit_length() - 1)

    def row_copy(tok_ref, dst_slot, r):
        tok = tok_ref[0, r]
        return pltpu.make_async_copy(h_hbm.at[pl.ds(tok, 1)], xbuf.at[dst_slot, pl.ds(r, 1)],
                                     sem.at[dst_slot])

    def issue_chunks(tok_ref, dst_slot, c0, n):
        def chunk(c, carry):
            base = (c0 + c) * GATHER_CHUNK

            def body(r, cc):
                row_copy(tok_ref, dst_slot, base + r).start()
                return cc
            lax.fori_loop(0, GATHER_CHUNK, body, 0, unroll=8)
            return carry
        lax.fori_loop(0, n, chunk, 0)

    def wait_chunks(dst_slot, n):
        def chunk(c, carry):
            pltpu.make_async_copy(h_hbm.at[pl.ds(0, GATHER_CHUNK)],
                                  xbuf.at[dst_slot, pl.ds(0, GATHER_CHUNK)], sem.at[dst_slot]).wait()
            return carry
        lax.fori_loop(0, n, chunk, 0)

    @pl.when(jnp.logical_and(v == 0, f == 0))
    def _():
        xbuf[...] = jnp.zeros_like(xbuf)
        issue_chunks(tokc_ref, 0, 0, n_chunks(0))

    @pl.when(jnp.logical_and(f == 0, nt > 0))
    def _():
        wait_chunks(slot, n_chunks(v))

    @pl.when(f == 0)
    def _():
        y_ref[...] = jnp.zeros_like(y_ref)

    nxt = jnp.minimum(v + 1, nv - 1)
    chunks_next = jnp.where(v + 1 < nv, n_chunks(nxt), 0)
    c0 = f * chunks_per_step
    issue_chunks(tokn_ref, 1 - slot, c0, jnp.clip(chunks_next - c0, 0, chunks_per_step))

    @pl.when(nt > 0)
    def _():
        wgb[...] = wg_ref[...].astype(BF16)
        wub[...] = wu_ref[...].astype(BF16)
        wdb[...] = wd_ref[...].astype(BF16)
        for tl in range(VISIT_ROWS // MOE_TILE):
            @pl.when(tl < nt)
            def _():
                rows = pl.ds(tl * MOE_TILE, MOE_TILE)
                x = xbuf[slot, rows, :].astype(BF16)
                a = jnp.dot(x, wgb[...], preferred_element_type=F32)
                u = jnp.dot(x, wub[...], preferred_element_type=F32)
                hmid = ((a * _sigmoid(a)) * u).astype(BF16)
                y_ref[rows, :] += jnp.dot(hmid, wdb[...], preferred_element_type=F32)


def _moe_experts(h2, row_tok, vis_e, vis_nt, vis_cnt, w_gate, w_up, w_down):
    n, d = h2.shape
    n_exp, _, dexp = w_gate.shape
    nv = vis_e.shape[0]
    nf = dexp // MOE_FCHUNK
    assert GATHER_CHUNK & (GATHER_CHUNK - 1) == 0 and VISIT_ROWS % (nf * GATHER_CHUNK) == 0
    tok3 = row_tok.reshape(nv, 1, VISIT_ROWS)

    def fidx(v, f, vnt):
        return jnp.where(vnt[v] > 0, f, nf - 1)

    grid_spec = pltpu.PrefetchScalarGridSpec(
        num_scalar_prefetch=3,
        grid=(nv, nf),
        in_specs=[
            pl.BlockSpec((None, 1, VISIT_ROWS), lambda v, f, ve, vnt, vcnt: (v, 0, 0),
                         memory_space=pltpu.SMEM),
            pl.BlockSpec((None, 1, VISIT_ROWS), lambda v, f, ve, vnt, vcnt: (jnp.minimum(v + 1, nv - 1), 0, 0),
                         memory_space=pltpu.SMEM),
            pl.BlockSpec(memory_space=pl.ANY),
            pl.BlockSpec((None, d, MOE_FCHUNK), lambda v, f, ve, vnt, vcnt: (ve[v], 0, fidx(v, f, vnt))),
            pl.BlockSpec((None, d, MOE_FCHUNK), lambda v, f, ve, vnt, vcnt: (ve[v], 0, fidx(v, f, vnt))),
            pl.BlockSpec((None, MOE_FCHUNK, d), lambda v, f, ve, vnt, vcnt: (ve[v], fidx(v, f, vnt), 0)),
        ],
        out_specs=pl.BlockSpec((VISIT_ROWS, d), lambda v, f, ve, vnt, vcnt: (v, 0)),
        scratch_shapes=[pltpu.VMEM((2, VISIT_ROWS, d), F32),
                        pltpu.SemaphoreType.DMA((2,)),
                        pltpu.VMEM((d, MOE_FCHUNK), BF16),
                        pltpu.VMEM((d, MOE_FCHUNK), BF16),
                        pltpu.VMEM((MOE_FCHUNK, d), BF16)],
    )
    return pl.pallas_call(
        functools.partial(_moe_kernel, nf=nf, nv=nv),
        out_shape=jax.ShapeDtypeStruct((nv * VISIT_ROWS, d), F32),
        grid_spec=grid_spec,
        compiler_params=_params("arbitrary", "arbitrary"),
        name="moe_experts",
    )(vis_e, vis_nt, vis_cnt, tok3, tok3, h2, w_gate, w_up, w_down)


def _expert_layout(e2, n_exp):
    n = e2.shape[1]
    m = 2 * n
    nv = m // VISIT_ROWS + n_exp
    flat_e = e2.T.reshape(-1)
    flat_tok = jnp.repeat(jnp.arange(n, dtype=I32), 2)
    onehot = (flat_e[:, None] == jnp.arange(n_exp, dtype=I32)[None, :]).astype(I32)
    csum = jnp.cumsum(onehot, axis=0)
    rank = jnp.take_along_axis(csum, flat_e[:, None], axis=1)[:, 0] - 1
    counts = csum[-1]
    n_vis = (counts + VISIT_ROWS - 1) // VISIT_ROWS
    cum_vis = jnp.cumsum(n_vis)
    vbase = cum_vis - n_vis
    dest = vbase[flat_e] * VISIT_ROWS + rank
    row_tok = jnp.zeros((nv * VISIT_ROWS,), I32).at[dest].set(flat_tok)

    n_used = cum_vis[-1]
    vid = jnp.arange(nv, dtype=I32)
    used = vid < n_used
    ve = jnp.minimum(jnp.searchsorted(cum_vis, vid, side="right").astype(I32), n_exp - 1)
    rem = counts[ve] - (vid - vbase[ve]) * VISIT_ROWS
    nt = jnp.clip((rem + MOE_TILE - 1) // MOE_TILE, 0, VISIT_ROWS // MOE_TILE)
    last = jnp.maximum(n_used - 1, 0)
    vis_nt = jnp.where(used, nt, 0).astype(I32)
    vis_cnt = jnp.where(used, jnp.clip(rem, 0, VISIT_ROWS), 0).astype(I32)
    vis_e = jnp.where(used, ve, ve[last]).astype(I32)
    return dest.reshape(n, 2), row_tok, vis_e, vis_nt, vis_cnt


def _final_kernel(posc_ref, posn_ref, x1_ref, gf_ref, w_ref, g_ref, ys_hbm, o_ref, ybuf, sem,
                  *, tm, nsteps):
    i = pl.program_id(0)
    slot = i % 2

    def row_copy(pos_ref, dst_slot, j):
        return pltpu.make_async_copy(ys_hbm.at[pl.ds(pos_ref[0, j], 1)],
                                     ybuf.at[dst_slot, pl.ds(j, 1)], sem.at[dst_slot])

    def issue(pos_ref, dst_slot):
        def body(j, c):
            row_copy(pos_ref, dst_slot, j).start()
            return c
        lax.fori_loop(0, 2 * tm, body, 0, unroll=8)

    @pl.when(i == 0)
    def _():
        issue(posc_ref, 0)

    @pl.when(i + 1 < nsteps)
    def _():
        issue(posn_ref, 1 - slot)

    pltpu.make_async_copy(ys_hbm.at[pl.ds(0, 2 * tm)], ybuf.at[slot], sem.at[slot]).wait()

    w = w_ref[...]
    moe = w[:, 0:1] * ybuf[slot, 0:tm, :] + w[:, 1:2] * ybuf[slot, tm:2 * tm, :]
    x = x1_ref[...] + gf_ref[...] * moe
    ms = jnp.mean(x * x, axis=-1, keepdims=True)
    o_ref[...] = x * lax.rsqrt(ms + EPS) * g_ref[...]


def _combine_final(x1, ys, pos, gate_w, mod3, g_final, seq):
    n, d = x1.shape
    tm = 256
    nsteps = n // tm
    per_b = seq // tm
    pos3 = pos.reshape(nsteps, tm, 2).transpose(0, 2, 1).reshape(nsteps, 1, 2 * tm)
    return pl.pallas_call(
        functools.partial(_final_kernel, tm=tm, nsteps=nsteps),
        out_shape=jax.ShapeDtypeStruct((n, d), F32),
        grid=(nsteps,),
        in_specs=[pl.BlockSpec((None, 1, 2 * tm), lambda i: (i, 0, 0), memory_space=pltpu.SMEM),
                  pl.BlockSpec((None, 1, 2 * tm), lambda i: (jnp.minimum(i + 1, nsteps - 1), 0, 0),
                               memory_space=pltpu.SMEM),
                  pl.BlockSpec((tm, d), lambda i: (i, 0)),
                  pl.BlockSpec((None, 1, d), lambda i: (i // per_b, 0, 5)),
                  pl.BlockSpec((tm, 2), lambda i: (i, 0)),
                  pl.BlockSpec((1, d), lambda i: (0, 0)),
                  pl.BlockSpec(memory_space=pl.ANY)],
        out_specs=pl.BlockSpec((tm, d), lambda i: (i, 0)),
        scratch_shapes=[pltpu.VMEM((2, 2 * tm, d), F32), pltpu.SemaphoreType.DMA((2,))],
        compiler_params=_params("arbitrary"),
        name="combine_final",
    )(pos3, pos3, x1, mod3, gate_w, g_final.reshape(1, d), ys)


def kernel(x, c, rel_bias_table, w_ada, b_ada, g_mix, w_in, lambda_q1, lambda_k1, lambda_q2,
           lambda_k2, g_subln, w_proj_a, w_proj_b, w_out, g_ffn, w_router_group, b_router_group,
           w_router_expert, b_router_expert, w_expert_gate, w_expert_up, w_expert_down, g_final):
    bsz, seq, d = x.shape
    n = bsz * seq
    depth = w_in.shape[0]
    assert depth == 1, "the MoE combine is fused with the final RMSNorm: one layer only"
    da_width = DA_HEADS * HEAD_DIM
    sb_width = SB_HEADS * HEAD_DIM
    qkv_cols = 3 * da_width + 3 * sb_width
    attn_tile = 256
    attn_qpt = seq // attn_tile
    n_exp = w_expert_gate.shape[1]
    xf = x.reshape(n, d)

    for l in range(depth):
        lam_init = 0.8 - 0.6 * math.exp(-0.3 * l)
        mod = _adaln_mod(c, w_ada[l:l + 1], b_ada[l])
        mod3 = mod.reshape(bsz, 1, N_MOD * d)

        h = _norm_modulate(xf.reshape(bsz, seq, d), g_mix[l], mod3, 1, 0).reshape(n, d)
        qkv = _in_proj(h, w_in[l:l + 1], 0, qkv_cols, BF16, gate=False)
        gates = _in_proj(h, w_in[l:l + 1], qkv_cols, 2 * d, F32, gate=True)
        qkv3 = qkv.reshape(bsz, seq, qkv_cols)

        lam_params = jnp.stack([lambda_q1[l], lambda_k1[l], lambda_q2[l], lambda_k2[l]]).astype(F32)
        bias = _bias_tiles(rel_bias_table, attn_tile)
        oa = _diff_attention(qkv3, lam_params, g_subln[l], bias, lam_init, attn_tile, attn_qpt)
        ob = _sb_attention(qkv3, 3 * DA_HEADS, attn_tile, attn_qpt)

        w_r = jnp.concatenate([w_router_group[l], w_router_expert[l]], axis=1).astype(F32).T
        w_r = jnp.pad(w_r, ((0, LANES - w_r.shape[0]), (0, 0)))
        wrh = w_r.astype(BF16)
        wrl = (w_r - wrh.astype(F32)).astype(BF16)
        b_r = jnp.concatenate([b_router_group[l], b_router_expert[l]]).astype(F32)
        b_r = jnp.pad(b_r, (0, LANES - b_r.shape[0])).reshape(LANES, 1)
        x1, h2, logits_t = _post_attention(
            oa.reshape(n, da_width), ob.reshape(n, sb_width), gates, xf, mod3, g_ffn[l],
            w_proj_a[l].astype(BF16), w_proj_b[l].astype(BF16), w_out[l].astype(BF16),
            wrh, wrl, b_r, seq)

        e2, g2 = _route(logits_t)
        pos, row_tok, vis_e, vis_nt, vis_cnt = _expert_layout(e2, n_exp)
        ys = _moe_experts(h2, row_tok, vis_e, vis_nt, vis_cnt,
                          w_expert_gate[l], w_expert_up[l], w_expert_down[l])
        out = _combine_final(x1, ys, pos, g2.T, mod3, g_final, seq)
    return out.reshape(bsz, seq, d)
```

```python
import functools
import math

import jax
import jax.numpy as jnp
from jax import lax
from jax.experimental import pallas as pl
from jax.experimental.pallas import tpu as pltpu

F32 = jnp.float32
BF16 = jnp.bfloat16
I32 = jnp.int32
EPS = 1e-6

DA_HEADS = 8
DA_HALF_DIM = 64
SB_HEADS = 8
HEAD_DIM = 128
REL_BUCKETS = 32
REL_MAX_DIST = 128
N_GROUPS = 8
EXPERTS_PER_GROUP = 8
N_MOD = 6

VMEM_LIMIT_BYTES = 56 * 1024 * 1024
LANES = 128

NT_DIMS = (((1,), (1,)), ((), ()))


def _params(*sem):
    return pltpu.CompilerParams(dimension_semantics=sem, vmem_limit_bytes=VMEM_LIMIT_BYTES)


def _sigmoid(v):
    return 1.0 / (1.0 + jnp.exp(-v))


def _mod_kernel(c_ref, w_ref, b_ref, o_ref):
    c = c_ref[...]
    s = (c * _sigmoid(c)).astype(BF16)
    o_ref[...] = jnp.dot(s, w_ref[...].astype(BF16), preferred_element_type=F32) + b_ref[...]


def _adaln_mod(c, w_ada, b_ada):
    bsz, d = c.shape
    ncol = w_ada.shape[-1]
    tn = 1024
    return pl.pallas_call(
        _mod_kernel,
        out_shape=jax.ShapeDtypeStruct((bsz, ncol), F32),
        grid=(ncol // tn,),
        in_specs=[pl.BlockSpec((bsz, d), lambda j: (0, 0)),
                  pl.BlockSpec((None, d, tn), lambda j: (0, 0, j)),
                  pl.BlockSpec((1, tn), lambda j: (0, j))],
        out_specs=pl.BlockSpec((bsz, tn), lambda j: (0, j)),
        compiler_params=_params("arbitrary"),
        name="adaln_mod",
    )(c, w_ada, b_ada.reshape(1, ncol))


def _hnorm_kernel(x_ref, g_ref, sc_ref, sh_ref, o_ref):
    x = x_ref[...]
    ms = jnp.mean(x * x, axis=-1, keepdims=True)
    y = x * lax.rsqrt(ms + EPS) * g_ref[...]
    o_ref[...] = (y * (1.0 + sc_ref[...]) + sh_ref[...]).astype(o_ref.dtype)


def _norm_modulate(x, g, mod3, scale_idx, shift_idx):
    bsz, s, d = x.shape
    ts = 512
    return pl.pallas_call(
        _hnorm_kernel,
        out_shape=jax.ShapeDtypeStruct((bsz, s, d), BF16),
        grid=(bsz, s // ts),
        in_specs=[pl.BlockSpec((None, ts, d), lambda b, i: (b, i, 0)),
                  pl.BlockSpec((1, d), lambda b, i: (0, 0)),
                  pl.BlockSpec((None, 1, d), lambda b, i: (b, 0, scale_idx)),
                  pl.BlockSpec((None, 1, d), lambda b, i: (b, 0, shift_idx))],
        out_specs=pl.BlockSpec((None, ts, d), lambda b, i: (b, i, 0)),
        compiler_params=_params("arbitrary", "arbitrary"),
        name="norm_modulate",
    )(x, g.reshape(1, d), mod3, mod3)


def _proj_kernel(h_ref, w_ref, o_ref, wb_ref, *, gate):
    @pl.when(pl.program_id(1) == 0)
    def _():
        wb_ref[...] = w_ref[...].astype(BF16)

    r = jnp.dot(h_ref[...], wb_ref[...], preferred_element_type=F32)
    if gate:
        r = _sigmoid(r)
    o_ref[...] = r.astype(o_ref.dtype)


def _in_proj(h2d, w_in, col0, ncols, out_dtype, gate):
    n, d = h2d.shape
    tn, tm = 1024, 1024
    jb = col0 // tn
    return pl.pallas_call(
        functools.partial(_proj_kernel, gate=gate),
        out_shape=jax.ShapeDtypeStruct((n, ncols), out_dtype),
        grid=(ncols // tn, n // tm),
        in_specs=[pl.BlockSpec((tm, d), lambda j, i: (i, 0)),
                  pl.BlockSpec((None, d, tn), lambda j, i: (0, 0, j + jb))],
        out_specs=pl.BlockSpec((tm, tn), lambda j, i: (i, j)),
        scratch_shapes=[pltpu.VMEM((d, tn), BF16)],
        compiler_params=_params("arbitrary", "arbitrary"),
        name="in_proj_gate" if gate else "in_proj_qkv",
    )(h2d, w_in)


def _rel_bucket(n):
    n = jnp.maximum(n, 0)
    max_exact = REL_BUCKETS // 2
    nf = jnp.maximum(n, 1).astype(F32)
    large = max_exact + (jnp.log(nf / max_exact) / math.log(REL_MAX_DIST / max_exact)
                         * (REL_BUCKETS - max_exact)).astype(I32)
    large = jnp.minimum(large, REL_BUCKETS - 1)
    return jnp.where(n < max_exact, n, large)


def _bias_rows(rel_table, t):
    assert 2 * t - (t - 1) >= REL_MAX_DIST, "far tiles must sit wholly in the last bucket"
    nb, h = rel_table.shape
    dist = jnp.arange(-(t - 1), 3 * t, dtype=I32)
    by_dist = rel_table[_rel_bucket(dist)].astype(F32).T
    rows = [jnp.pad(by_dist[:, delta * t:delta * t + 2 * t - 1][:, ::-1], ((0, 0), (0, 1)))
            for delta in range(2)]
    rows.append(jnp.broadcast_to(by_dist[:, -1:], (h, 2 * t)))
    return jnp.stack(rows, axis=1)


def _lane_chunks(x):
    return [x[:, c * LANES:(c + 1) * LANES] for c in range(x.shape[1] // LANES)]


def _diff_attn_kernel(lam_ref, g_ref, brow_ref, q_ref, k_ref, v_ref, o_ref, s1_ref, s2_ref, bias_ref,
                      *, t, nsteps, qpt, lam_init):
    step = pl.program_id(2)
    lp = lam_ref[...]
    lam = (jnp.exp(jnp.sum(lp[0:1] * lp[1:2], axis=-1, keepdims=True))
           - jnp.exp(jnp.sum(lp[2:3] * lp[3:4], axis=-1, keepdims=True)) + lam_init)
    lane = lax.broadcasted_iota(I32, (t, HEAD_DIM), 1)

    qrow = lax.broadcasted_iota(I32, (t, t), 0)
    kcol = lax.broadcasted_iota(I32, (t, t), 1)
    for delta in range(2):
        gen = jnp.broadcast_to(brow_ref[delta:delta + 1, :], (t, 2 * t))
        tile = pltpu.roll(gen, t + 1, 1, stride=1, stride_axis=0)[:, :t]
        if delta == 0:
            tile = jnp.where(kcol <= qrow, tile, -jnp.inf)
        bias_ref[delta] = tile
    bias_ref[2] = jnp.broadcast_to(brow_ref[2:3, :t], (t, t))

    def scores(qh, s_ref, nk):
        mx = None
        for j in range(nk):
            s = (lax.dot_general(qh, k_ref[j * t:(j + 1) * t, :], NT_DIMS, preferred_element_type=F32)
                 + bias_ref[min(nk - 1 - j, 2)])
            s_ref[:, j * t:(j + 1) * t] = s
            for ch in _lane_chunks(s):
                mx = ch if mx is None else jnp.maximum(mx, ch)
        return jnp.broadcast_to(jnp.max(mx, axis=-1, keepdims=True), (t, LANES))

    def softmax_pv(s_ref, mb, nk):
        lsum = jnp.zeros((t, LANES), F32)
        acc = jnp.zeros((t, HEAD_DIM), F32)
        for j in range(nk):
            ps = [jnp.exp(ch - mb) for ch in _lane_chunks(s_ref[:, j * t:(j + 1) * t])]
            for p in ps:
                lsum = lsum + p
            acc = acc + jnp.dot(jnp.concatenate(ps, axis=1).astype(BF16), v_ref[j * t:(j + 1) * t, :],
                                preferred_element_type=F32)
        return acc / jnp.sum(lsum, axis=-1, keepdims=True)

    def q_tile(lt, nk):
        rows = slice(lt * t, (lt + 1) * t)
        q = q_ref[rows, :] * jnp.asarray(DA_HALF_DIM ** -0.5, BF16)
        q1 = jnp.where(lane < DA_HALF_DIM, q, jnp.zeros_like(q))
        q2 = jnp.where(lane >= DA_HALF_DIM, q, jnp.zeros_like(q))
        s1, s2 = s1_ref.at[lt % 2], s2_ref.at[lt % 2]
        m1 = scores(q1, s1, nk)
        m2 = scores(q2, s2, nk)
        o = softmax_pv(s1, m1, nk) - lam * softmax_pv(s2, m2, nk)
        ms = jnp.mean(o * o, axis=-1, keepdims=True)
        o = (o * lax.rsqrt(ms + EPS) * g_ref[...]) * (1.0 - lam_init)
        o_ref[rows, :] = o.astype(o_ref.dtype)

    def variant(first_tile):
        for lt in range(qpt):
            q_tile(lt, first_tile + lt + 1)

    for i in range(nsteps):
        pl.when(step == i)(functools.partial(variant, i * qpt))


def _diff_attention(qkv3, lam_params, g_subln, bias, lam_init, t, qpt):
    bsz, s, _ = qkv3.shape
    h = DA_HEADS
    nsteps = s // (t * qpt)
    return pl.pallas_call(
        functools.partial(_diff_attn_kernel, t=t, nsteps=nsteps, qpt=qpt, lam_init=lam_init),
        out_shape=jax.ShapeDtypeStruct((bsz, s, h * HEAD_DIM), BF16),
        scratch_shapes=[pltpu.VMEM((2, t, s), F32), pltpu.VMEM((2, t, s), F32), pltpu.VMEM((3, t, t), F32)],
        grid=(bsz, h, nsteps),
        in_specs=[pl.BlockSpec((4, DA_HALF_DIM), lambda b, hh, i: (0, 0)),
                  pl.BlockSpec((1, HEAD_DIM), lambda b, hh, i: (0, 0)),
                  pl.BlockSpec((None, 3, 2 * t), lambda b, hh, i: (hh, 0, 0)),
                  pl.BlockSpec((None, t * qpt, HEAD_DIM), lambda b, hh, i: (b, i, hh)),
                  pl.BlockSpec((None, s, HEAD_DIM), lambda b, hh, i: (b, 0, h + hh)),
                  pl.BlockSpec((None, s, HEAD_DIM), lambda b, hh, i: (b, 0, 2 * h + hh))],
        out_specs=pl.BlockSpec((None, t * qpt, HEAD_DIM), lambda b, hh, i: (b, i, hh)),
        compiler_params=_params("arbitrary", "arbitrary", "arbitrary"),
        name="diff_attention",
    )(lam_params, g_subln.reshape(1, HEAD_DIM), bias, qkv3, qkv3, qkv3)


def _sb_attn_kernel(q_ref, k_ref, v_ref, o_ref, e_ref, *, t, nsteps, qpt, scale):
    step = pl.program_id(2)
    row = lax.broadcasted_iota(I32, (t, t), 0)
    col = lax.broadcasted_iota(I32, (t, t), 1)
    neg_tri = jnp.where(row >= col, -1.0, 0.0).astype(BF16)
    past = col < row

    def q_tile(lt, nk):
        rows = slice(lt * t, (lt + 1) * t)
        q = q_ref[rows, :]
        e = e_ref.at[lt % 2]
        totals = []
        for j in range(nk):
            z = lax.dot_general(q, k_ref[j * t:(j + 1) * t, :], NT_DIMS, preferred_element_type=F32) * scale
            softplus = jnp.maximum(z, 0.0) + jnp.log(1.0 + jnp.exp(-jnp.abs(z)))
            if j == nk - 1:
                softplus = jnp.where(past, softplus, 0.0)
            incl = jnp.dot(softplus.astype(BF16), neg_tri, preferred_element_type=F32)
            e[:, j * t:(j + 1) * t] = z + incl
            totals.append(incl[:, 0:1])

        acc = jnp.zeros((t, HEAD_DIM), F32)
        later = jnp.zeros((t, 1), F32)
        for j in reversed(range(nk)):
            lb = jnp.broadcast_to(later, (t, LANES))
            ws = [jnp.exp(ch + lb) for ch in _lane_chunks(e[:, j * t:(j + 1) * t])]
            w = jnp.concatenate(ws, axis=1)
            if j == nk - 1:
                w = jnp.where(past, w, 0.0)
            acc = acc + jnp.dot(w.astype(BF16), v_ref[j * t:(j + 1) * t, :], preferred_element_type=F32)
            later = later + totals[j]
        o_ref[rows, :] = acc.astype(o_ref.dtype)

    def variant(first_tile):
        for lt in range(qpt):
            q_tile(lt, first_tile + lt + 1)

    for i in range(nsteps):
        pl.when(step == i)(functools.partial(variant, i * qpt))


def _sb_attention(qkv3, col_block0, t, qpt):
    bsz, s, _ = qkv3.shape
    h = SB_HEADS
    nsteps = s // (t * qpt)
    return pl.pallas_call(
        functools.partial(_sb_attn_kernel, t=t, nsteps=nsteps, qpt=qpt, scale=HEAD_DIM ** -0.5),
        out_shape=jax.ShapeDtypeStruct((bsz, s, h * HEAD_DIM), BF16),
        scratch_shapes=[pltpu.VMEM((2, t, s), F32)],
        grid=(bsz, h, nsteps),
        in_specs=[pl.BlockSpec((None, t * qpt, HEAD_DIM), lambda b, hh, i: (b, i, col_block0 + hh)),
                  pl.BlockSpec((None, s, HEAD_DIM), lambda b, hh, i: (b, 0, col_block0 + h + hh)),
                  pl.BlockSpec((None, s, HEAD_DIM), lambda b, hh, i: (b, 0, col_block0 + 2 * h + hh))],
        out_specs=pl.BlockSpec((None, t * qpt, HEAD_DIM), lambda b, hh, i: (b, i, hh)),
        compiler_params=_params("arbitrary", "arbitrary", "arbitrary"),
        name="sb_attention",
    )(qkv3, qkv3, qkv3)


def _merge_kernel(oa_ref, ob_ref, sa_ref, sb_ref, wa_ref, wb_ref, m_ref):
    merged = (sa_ref[...] * jnp.dot(oa_ref[...], wa_ref[...], preferred_element_type=F32)
              + sb_ref[...] * jnp.dot(ob_ref[...], wb_ref[...], preferred_element_type=F32))
    m_ref[...] = merged.astype(m_ref.dtype)


def _merge_branches(oa, ob, gates, wa, wb):
    n, width = oa.shape
    d = wa.shape[1]
    tm = 512
    const = lambda shape: pl.BlockSpec(shape, lambda i: (0,) * len(shape), pipeline_mode=pl.Buffered(1))
    return pl.pallas_call(
        _merge_kernel,
        out_shape=jax.ShapeDtypeStruct((n, d), BF16),
        grid=(n // tm,),
        in_specs=[pl.BlockSpec((tm, width), lambda i: (i, 0)),
                  pl.BlockSpec((tm, width), lambda i: (i, 0)),
                  pl.BlockSpec((tm, d), lambda i: (i, 0)),
                  pl.BlockSpec((tm, d), lambda i: (i, 1)),
                  const((width, d)), const((width, d))],
        out_specs=pl.BlockSpec((tm, d), lambda i: (i, 0)),
        compiler_params=_params("arbitrary"),
        name="merge_branches",
    )(oa, ob, gates, gates, wa, wb)


def _post_kernel(m_ref, x_ref, gm_ref, scf_ref, shf_ref, gffn_ref,
                 wo_ref, wrh_ref, wrl_ref, br_ref,
                 x1_ref, h2_ref, lg_ref):
    y = jnp.dot(m_ref[...], wo_ref[...], preferred_element_type=F32)
    x1 = x_ref[...] + gm_ref[...] * y
    x1_ref[...] = x1
    ms = jnp.mean(x1 * x1, axis=-1, keepdims=True)
    h2 = (x1 * lax.rsqrt(ms + EPS) * gffn_ref[...]) * (1.0 + scf_ref[...]) + shf_ref[...]
    h2_ref[...] = h2
    hb = h2.astype(BF16)
    hl = (h2 - hb.astype(F32)).astype(BF16)
    wrh = wrh_ref[...]
    lg = (lax.dot_general(wrh, hb, NT_DIMS, preferred_element_type=F32)
          + lax.dot_general(wrh, hl, NT_DIMS, preferred_element_type=F32)
          + lax.dot_general(wrl_ref[...], hb, NT_DIMS, preferred_element_type=F32))
    lg_ref[...] = lg + br_ref[...]


def _post_attention(merged, x2d, mod3, g_ffn, wo, wrh, wrl, br, seq):
    n, d = x2d.shape
    tm = 512
    per_b = seq // tm
    const = lambda shape: pl.BlockSpec(shape, lambda i: (0,) * len(shape), pipeline_mode=pl.Buffered(1))
    modspec = lambda idx: pl.BlockSpec((None, 1, d), lambda i: (i // per_b, 0, idx))
    return pl.pallas_call(
        _post_kernel,
        out_shape=(jax.ShapeDtypeStruct((n, d), F32),
                   jax.ShapeDtypeStruct((n, d), F32),
                   jax.ShapeDtypeStruct((LANES, n), F32)),
        grid=(n // tm,),
        in_specs=[pl.BlockSpec((tm, d), lambda i: (i, 0)),
                  pl.BlockSpec((tm, d), lambda i: (i, 0)),
                  modspec(2), modspec(4), modspec(3),
                  const((1, d)),
                  const((d, d)),
                  const((LANES, d)), const((LANES, d)), const((LANES, 1))],
        out_specs=(pl.BlockSpec((tm, d), lambda i: (i, 0)),
                   pl.BlockSpec((tm, d), lambda i: (i, 0)),
                   pl.BlockSpec((LANES, tm), lambda i: (0, i))),
        compiler_params=_params("arbitrary"),
        name="post_attention",
    )(merged, x2d, mod3, mod3, mod3, g_ffn.reshape(1, d), wo, wrh, wrl, br)


def _first_index_of_max(vals, iota, nrows):
    mx = jnp.max(vals, axis=0, keepdims=True)
    idx = jnp.min(jnp.where(vals == mx, iota, nrows), axis=0, keepdims=True)
    return mx, idx


def _route_kernel(lg_ref, e_ref, w_ref):
    g = N_GROUPS
    epg = EXPERTS_PER_GROUP
    lg = lg_ref[...]
    gl = lg[0:g, :]
    iota = lax.broadcasted_iota(I32, gl.shape, 0)
    gmax, gidx = _first_index_of_max(gl, iota, g)
    p_g = 1.0 / jnp.sum(jnp.exp(gl - gmax), axis=0, keepdims=True)

    esel = jnp.zeros((epg, lg.shape[1]), F32)
    for gi in range(g):
        esel = jnp.where(gidx == gi, lg[g + gi * epg:g + (gi + 1) * epg, :], esel)
    emax = jnp.max(esel, axis=0, keepdims=True)
    ex = jnp.exp(esel - emax)
    prob = ex / jnp.sum(ex, axis=0, keepdims=True)

    p0, i0 = _first_index_of_max(prob, iota, epg)
    rest = jnp.where(iota == i0, -1.0, prob)
    p1, i1 = _first_index_of_max(rest, iota, epg)
    tot = p0 + p1
    e_ref[0:1, :] = gidx * epg + i0
    e_ref[1:2, :] = gidx * epg + i1
    w_ref[0:1, :] = p_g * (p0 / tot)
    w_ref[1:2, :] = p_g * (p1 / tot)


def _route(logits_t):
    rows, n = logits_t.shape
    tn = 1024
    return pl.pallas_call(
        _route_kernel,
        out_shape=(jax.ShapeDtypeStruct((2, n), I32), jax.ShapeDtypeStruct((2, n), F32)),
        grid=(n // tn,),
        in_specs=[pl.BlockSpec((rows, tn), lambda i: (0, i))],
        out_specs=(pl.BlockSpec((2, tn), lambda i: (0, i)), pl.BlockSpec((2, tn), lambda i: (0, i))),
        compiler_params=_params("arbitrary"),
        name="route",
    )(logits_t)


VISIT_ROWS = 512
MOE_TILE = 256
MOE_FCHUNK = 512
GATHER_CHUNK = 32


def _moe_kernel(ve_ref, vnt_ref, vcnt_ref,
                tokc_ref, tokn_ref, h_hbm, wg_ref, wu_ref, wd_ref,
                y_ref,
                xbuf, sem, wgb, wub, wdb, *, nf, nv):
    v = pl.program_id(0)
    f = pl.program_id(1)
    slot = v % 2
    nt = vnt_ref[v]
    chunks_per_step = VISIT_ROWS // nf // GATHER_CHUNK

    def n_chunks(vv):
        return lax.shift_right_logical(vcnt_ref[vv] + (GATHER_CHUNK - 1), GATHER_CHUNK.bit_length() - 1)

    def row_copy(tok_ref, dst_slot, r):
        tok = tok_ref[0, r]
        return pltpu.make_async_copy(h_hbm.at[pl.ds(tok, 1)], xbuf.at[dst_slot, pl.ds(r, 1)],
                                     sem.at[dst_slot])

    def issue_chunks(tok_ref, dst_slot, c0, n):
        def chunk(c, carry):
            base = (c0 + c) * GATHER_CHUNK

            def body(r, cc):
                row_copy(tok_ref, dst_slot, base + r).start()
                return cc
            lax.fori_loop(0, GATHER_CHUNK, body, 0, unroll=8)
            return carry
        lax.fori_loop(0, n, chunk, 0)

    def wait_chunks(dst_slot, n):
        def chunk(c, carry):
            pltpu.make_async_copy(h_hbm.at[pl.ds(0, GATHER_CHUNK)],
                                  xbuf.at[dst_slot, pl.ds(0, GATHER_CHUNK)], sem.at[dst_slot]).wait()
            return carry
        lax.fori_loop(0, n, chunk, 0)

    @pl.when(jnp.logical_and(v == 0, f == 0))
    def _():
        xbuf[...] = jnp.zeros_like(xbuf)
        issue_chunks(tokc_ref, 0, 0, n_chunks(0))

    @pl.when(jnp.logical_and(f == 0, nt > 0))
    def _():
        wait_chunks(slot, n_chunks(v))

    @pl.when(f == 0)
    def _():
        y_ref[...] = jnp.zeros_like(y_ref)

    nxt = jnp.minimum(v + 1, nv - 1)
    chunks_next = jnp.where(v + 1 < nv, n_chunks(nxt), 0)
    c0 = f * chunks_per_step
    issue_chunks(tokn_ref, 1 - slot, c0, jnp.clip(chunks_next - c0, 0, chunks_per_step))

    @pl.when(nt > 0)
    def _():
        wgb[...] = wg_ref[...].astype(BF16)
        wub[...] = wu_ref[...].astype(BF16)
        wdb[...] = wd_ref[...].astype(BF16)
        for tl in range(VISIT_ROWS // MOE_TILE):
            @pl.when(tl < nt)
            def _():
                rows = pl.ds(tl * MOE_TILE, MOE_TILE)
                x = xbuf[slot, rows, :].astype(BF16)
                a = jnp.dot(x, wgb[...], preferred_element_type=F32)
                u = jnp.dot(x, wub[...], preferred_element_type=F32)
                hmid = ((a * _sigmoid(a)) * u).astype(BF16)
                y_ref[rows, :] += jnp.dot(hmid, wdb[...], preferred_element_type=F32)


def _moe_experts(h2, row_tok, vis_e, vis_nt, vis_cnt, w_gate, w_up, w_down):
    n, d = h2.shape
    n_exp, _, dexp = w_gate.shape
    nv = vis_e.shape[0]
    nf = dexp // MOE_FCHUNK
    assert GATHER_CHUNK & (GATHER_CHUNK - 1) == 0 and VISIT_ROWS % (nf * GATHER_CHUNK) == 0
    tok3 = row_tok.reshape(nv, 1, VISIT_ROWS)

    def fidx(v, f, vnt):
        return jnp.where(vnt[v] > 0, f, nf - 1)

    grid_spec = pltpu.PrefetchScalarGridSpec(
        num_scalar_prefetch=3,
        grid=(nv, nf),
        in_specs=[
            pl.BlockSpec((None, 1, VISIT_ROWS), lambda v, f, ve, vnt, vcnt: (v, 0, 0),
                         memory_space=pltpu.SMEM),
            pl.BlockSpec((None, 1, VISIT_ROWS), lambda v, f, ve, vnt, vcnt: (jnp.minimum(v + 1, nv - 1), 0, 0),
                         memory_space=pltpu.SMEM),
            pl.BlockSpec(memory_space=pl.ANY),
            pl.BlockSpec((None, d, MOE_FCHUNK), lambda v, f, ve, vnt, vcnt: (ve[v], 0, fidx(v, f, vnt))),
            pl.BlockSpec((None, d, MOE_FCHUNK), lambda v, f, ve, vnt, vcnt: (ve[v], 0, fidx(v, f, vnt))),
            pl.BlockSpec((None, MOE_FCHUNK, d), lambda v, f, ve, vnt, vcnt: (ve[v], fidx(v, f, vnt), 0)),
        ],
        out_specs=pl.BlockSpec((VISIT_ROWS, d), lambda v, f, ve, vnt, vcnt: (v, 0)),
        scratch_shapes=[pltpu.VMEM((2, VISIT_ROWS, d), F32),
                        pltpu.SemaphoreType.DMA((2,)),
                        pltpu.VMEM((d, MOE_FCHUNK), BF16),
                        pltpu.VMEM((d, MOE_FCHUNK), BF16),
                        pltpu.VMEM((MOE_FCHUNK, d), BF16)],
    )
    return pl.pallas_call(
        functools.partial(_moe_kernel, nf=nf, nv=nv),
        out_shape=jax.ShapeDtypeStruct((nv * VISIT_ROWS, d), F32),
        grid_spec=grid_spec,
        compiler_params=_params("arbitrary", "arbitrary"),
        name="moe_experts",
    )(vis_e, vis_nt, vis_cnt, tok3, tok3, h2, w_gate, w_up, w_down)


RANK_BLOCK = 256


def _slot_kernel(e_ref, dest_ref, cnt_ref, rank_ref, *, n_exp):
    nk, n = e_ref.shape
    eids = lax.broadcasted_iota(I32, (n_exp, RANK_BLOCK), 0)
    row = lax.broadcasted_iota(I32, (RANK_BLOCK, RANK_BLOCK), 0)
    col = lax.broadcasted_iota(I32, (RANK_BLOCK, RANK_BLOCK), 1)
    before = (row < col).astype(BF16)
    carry = jnp.zeros((n_exp, 1), F32)
    for k in range(nk):
        for c0 in range(0, n, RANK_BLOCK):
            hit = eids == e_ref[k:k + 1, c0:c0 + RANK_BLOCK]
            onehot = jnp.where(hit, 1.0, 0.0)
            earlier = jnp.dot(onehot.astype(BF16), before, preferred_element_type=F32) + carry
            rank = jnp.sum(jnp.where(hit, earlier, 0.0), axis=0, keepdims=True)
            rank_ref[k:k + 1, c0:c0 + RANK_BLOCK] = rank.astype(I32)
            carry = carry + jnp.sum(onehot, axis=1, keepdims=True)
    cnt_ref[...] = jnp.broadcast_to(carry, cnt_ref.shape).astype(I32)

    n_vis = lax.shift_right_logical(carry.astype(I32) + (VISIT_ROWS - 1),
                                    VISIT_ROWS.bit_length() - 1).astype(F32)
    ei = lax.broadcasted_iota(I32, (n_exp, n_exp), 0)
    ej = lax.broadcasted_iota(I32, (n_exp, n_exp), 1)
    lower = (ej < ei).astype(BF16)
    first_row = jnp.dot(lower, jnp.broadcast_to(n_vis, (n_exp, LANES)).astype(BF16),
                        preferred_element_type=F32)[:, 0:1] * float(VISIT_ROWS)
    for k in range(nk):
        for c0 in range(0, n, RANK_BLOCK):
            hit = eids == e_ref[k:k + 1, c0:c0 + RANK_BLOCK]
            base = jnp.sum(jnp.where(hit, first_row, 0.0), axis=0, keepdims=True)
            dest_ref[k:k + 1, c0:c0 + RANK_BLOCK] = rank_ref[k:k + 1, c0:c0 + RANK_BLOCK] + base.astype(I32)


def _assignment_slots(e2, n_exp):
    nk, n = e2.shape
    assert (n // VISIT_ROWS * nk + n_exp) <= 256, "visit counts must stay exact in bf16"
    return pl.pallas_call(
        functools.partial(_slot_kernel, n_exp=n_exp),
        out_shape=(jax.ShapeDtypeStruct((nk, n), I32), jax.ShapeDtypeStruct((n_exp, LANES), I32)),
        scratch_shapes=[pltpu.VMEM((nk, n), I32)],
        compiler_params=pltpu.CompilerParams(vmem_limit_bytes=VMEM_LIMIT_BYTES),
        name="assignment_slots",
    )(e2)


def _expert_layout(e2, n_exp):
    n = e2.shape[1]
    m = 2 * n
    nv = m // VISIT_ROWS + n_exp
    dest2, cnt = _assignment_slots(e2, n_exp)
    flat_tok = jnp.tile(jnp.arange(n, dtype=I32), 2)
    dest = dest2.reshape(-1)
    counts = cnt[:, 0]
    n_vis = (counts + VISIT_ROWS - 1) // VISIT_ROWS
    cum_vis = jnp.cumsum(n_vis)
    vbase = cum_vis - n_vis
    row_tok = jnp.zeros((nv * VISIT_ROWS,), I32).at[dest].set(
        flat_tok, unique_indices=True, mode="promise_in_bounds")

    n_used = cum_vis[-1]
    vid = jnp.arange(nv, dtype=I32)
    used = vid < n_used
    ve = jnp.minimum(jnp.sum((cum_vis[None, :] <= vid[:, None]).astype(I32), axis=1), n_exp - 1)
    rem = counts[ve] - (vid - vbase[ve]) * VISIT_ROWS
    nt = jnp.clip((rem + MOE_TILE - 1) // MOE_TILE, 0, VISIT_ROWS // MOE_TILE)
    last = jnp.maximum(n_used - 1, 0)
    vis_nt = jnp.where(used, nt, 0).astype(I32)
    vis_cnt = jnp.where(used, jnp.clip(rem, 0, VISIT_ROWS), 0).astype(I32)
    vis_e = jnp.where(used, ve, ve[last]).astype(I32)
    return dest2, row_tok, vis_e, vis_nt, vis_cnt


def _final_kernel(posc_ref, posn_ref, x1_ref, gf_ref, w_ref, g_ref, ys_hbm, o_ref, ybuf, sem,
                  *, tm, nsteps):
    i = pl.program_id(0)
    slot = i % 2

    def row_copy(pos_ref, dst_slot, j):
        return pltpu.make_async_copy(ys_hbm.at[pl.ds(pos_ref[0, j], 1)],
                                     ybuf.at[dst_slot, pl.ds(j, 1)], sem.at[dst_slot])

    def issue(pos_ref, dst_slot):
        def body(j, c):
            row_copy(pos_ref, dst_slot, j).start()
            return c
        lax.fori_loop(0, 2 * tm, body, 0, unroll=8)

    @pl.when(i == 0)
    def _():
        issue(posc_ref, 0)

    @pl.when(i + 1 < nsteps)
    def _():
        issue(posn_ref, 1 - slot)

    pltpu.make_async_copy(ys_hbm.at[pl.ds(0, 2 * tm)], ybuf.at[slot], sem.at[slot]).wait()

    w = w_ref[...]
    moe = w[:, 0:1] * ybuf[slot, 0:tm, :] + w[:, 1:2] * ybuf[slot, tm:2 * tm, :]
    x = x1_ref[...] + gf_ref[...] * moe
    ms = jnp.mean(x * x, axis=-1, keepdims=True)
    o_ref[...] = x * lax.rsqrt(ms + EPS) * g_ref[...]


def _combine_final(x1, ys, pos, gate_w, mod3, g_final, seq):
    n, d = x1.shape
    tm = 256
    nsteps = n // tm
    per_b = seq // tm
    pos3 = pos.reshape(2, nsteps, tm).transpose(1, 0, 2).reshape(nsteps, 1, 2 * tm)
    return pl.pallas_call(
        functools.partial(_final_kernel, tm=tm, nsteps=nsteps),
        out_shape=jax.ShapeDtypeStruct((n, d), F32),
        grid=(nsteps,),
        in_specs=[pl.BlockSpec((None, 1, 2 * tm), lambda i: (i, 0, 0), memory_space=pltpu.SMEM),
                  pl.BlockSpec((None, 1, 2 * tm), lambda i: (jnp.minimum(i + 1, nsteps - 1), 0, 0),
                               memory_space=pltpu.SMEM),
                  pl.BlockSpec((tm, d), lambda i: (i, 0)),
                  pl.BlockSpec((None, 1, d), lambda i: (i // per_b, 0, 5)),
                  pl.BlockSpec((tm, 2), lambda i: (i, 0)),
                  pl.BlockSpec((1, d), lambda i: (0, 0)),
                  pl.BlockSpec(memory_space=pl.ANY)],
        out_specs=pl.BlockSpec((tm, d), lambda i: (i, 0)),
        scratch_shapes=[pltpu.VMEM((2, 2 * tm, d), F32), pltpu.SemaphoreType.DMA((2,))],
        compiler_params=_params("arbitrary"),
        name="combine_final",
    )(pos3, pos3, x1, mod3, gate_w, g_final.reshape(1, d), ys)


def kernel(x, c, rel_bias_table, w_ada, b_ada, g_mix, w_in, lambda_q1, lambda_k1, lambda_q2,
           lambda_k2, g_subln, w_proj_a, w_proj_b, w_out, g_ffn, w_router_group, b_router_group,
           w_router_expert, b_router_expert, w_expert_gate, w_expert_up, w_expert_down, g_final):
    bsz, seq, d = x.shape
    n = bsz * seq
    depth = w_in.shape[0]
    assert depth == 1, "the MoE combine is fused with the final RMSNorm: one layer only"
    da_width = DA_HEADS * HEAD_DIM
    sb_width = SB_HEADS * HEAD_DIM
    qkv_cols = 3 * da_width + 3 * sb_width
    attn_tile = 256
    attn_qpt = seq // attn_tile
    n_exp = w_expert_gate.shape[1]
    xf = x.reshape(n, d)

    for l in range(depth):
        lam_init = 0.8 - 0.6 * math.exp(-0.3 * l)
        mod = _adaln_mod(c, w_ada[l:l + 1], b_ada[l])
        mod3 = mod.reshape(bsz, 1, N_MOD * d)

        h = _norm_modulate(xf.reshape(bsz, seq, d), g_mix[l], mod3, 1, 0).reshape(n, d)
        qkv = _in_proj(h, w_in[l:l + 1], 0, qkv_cols, BF16, gate=False)
        gates = _in_proj(h, w_in[l:l + 1], qkv_cols, 2 * d, F32, gate=True)
        qkv3 = qkv.reshape(bsz, seq, qkv_cols)

        lam_params = jnp.stack([lambda_q1[l], lambda_k1[l], lambda_q2[l], lambda_k2[l]]).astype(F32)
        bias = _bias_rows(rel_bias_table, attn_tile)
        oa = _diff_attention(qkv3, lam_params, g_subln[l], bias, lam_init, attn_tile, attn_qpt)
        ob = _sb_attention(qkv3, 3 * DA_HEADS, attn_tile, attn_qpt)

        w_r = jnp.concatenate([w_router_group[l], w_router_expert[l]], axis=1).astype(F32).T
        w_r = jnp.pad(w_r, ((0, LANES - w_r.shape[0]), (0, 0)))
        wrh = w_r.astype(BF16)
        wrl = (w_r - wrh.astype(F32)).astype(BF16)
        b_r = jnp.concatenate([b_router_group[l], b_router_expert[l]]).astype(F32)
        b_r = jnp.pad(b_r, (0, LANES - b_r.shape[0])).reshape(LANES, 1)
        merged = _merge_branches(oa.reshape(n, da_width), ob.reshape(n, sb_width), gates,
                                 w_proj_a[l].astype(BF16), w_proj_b[l].astype(BF16))
        x1, h2, logits_t = _post_attention(merged, xf, mod3, g_ffn[l], w_out[l].astype(BF16),
                                           wrh, wrl, b_r, seq)

        e2, g2 = _route(logits_t)
        pos, row_tok, vis_e, vis_nt, vis_cnt = _expert_layout(e2, n_exp)
        ys = _moe_experts(h2, row_tok, vis_e, vis_nt, vis_cnt,
                          w_expert_gate[l], w_expert_up[l], w_expert_down[l])
        out = _combine_final(x1, ys, pos, g2.T, mod3, g_final, seq)
    return out.reshape(bsz, seq, d)
```

```python
import functools
import math

import jax
import jax.numpy as jnp
from jax import lax
from jax.experimental import pallas as pl
from jax.experimental.pallas import tpu as pltpu

F32 = jnp.float32
BF16 = jnp.bfloat16
I32 = jnp.int32
EPS = 1e-6

DA_HEADS = 8
DA_HALF_DIM = 64
SB_HEADS = 8
HEAD_DIM = 128
REL_BUCKETS = 32
REL_MAX_DIST = 128
N_GROUPS = 8
EXPERTS_PER_GROUP = 8
N_MOD = 6

VMEM_LIMIT_BYTES = 56 * 1024 * 1024
LANES = 128

NT_DIMS = (((1,), (1,)), ((), ()))


def _params(*sem):
    return pltpu.CompilerParams(dimension_semantics=sem, vmem_limit_bytes=VMEM_LIMIT_BYTES)


def _sigmoid(v):
    return 1.0 / (1.0 + jnp.exp(-v))


def _mod_kernel(c_ref, w_ref, b_ref, o_ref):
    c = c_ref[...]
    s = (c * _sigmoid(c)).astype(BF16)
    o_ref[...] = jnp.dot(s, w_ref[...].astype(BF16), preferred_element_type=F32) + b_ref[...]


def _adaln_mod(c, w_ada, b_ada):
    bsz, d = c.shape
    ncol = w_ada.shape[-1]
    tn = 1024
    return pl.pallas_call(
        _mod_kernel,
        out_shape=jax.ShapeDtypeStruct((bsz, ncol), F32),
        grid=(ncol // tn,),
        in_specs=[pl.BlockSpec((bsz, d), lambda j: (0, 0)),
                  pl.BlockSpec((None, d, tn), lambda j: (0, 0, j)),
                  pl.BlockSpec((1, tn), lambda j: (0, j))],
        out_specs=pl.BlockSpec((bsz, tn), lambda j: (0, j)),
        compiler_params=_params("arbitrary"),
        name="adaln_mod",
    )(c, w_ada, b_ada.reshape(1, ncol))


def _hnorm_kernel(x_ref, g_ref, sc_ref, sh_ref, o_ref):
    x = x_ref[...]
    ms = jnp.mean(x * x, axis=-1, keepdims=True)
    y = x * lax.rsqrt(ms + EPS) * g_ref[...]
    o_ref[...] = (y * (1.0 + sc_ref[...]) + sh_ref[...]).astype(o_ref.dtype)


def _norm_modulate(x, g, mod3, scale_idx, shift_idx):
    bsz, s, d = x.shape
    ts = 512
    return pl.pallas_call(
        _hnorm_kernel,
        out_shape=jax.ShapeDtypeStruct((bsz, s, d), BF16),
        grid=(bsz, s // ts),
        in_specs=[pl.BlockSpec((None, ts, d), lambda b, i: (b, i, 0)),
                  pl.BlockSpec((1, d), lambda b, i: (0, 0)),
                  pl.BlockSpec((None, 1, d), lambda b, i: (b, 0, scale_idx)),
                  pl.BlockSpec((None, 1, d), lambda b, i: (b, 0, shift_idx))],
        out_specs=pl.BlockSpec((None, ts, d), lambda b, i: (b, i, 0)),
        compiler_params=_params("arbitrary", "arbitrary"),
        name="norm_modulate",
    )(x, g.reshape(1, d), mod3, mod3)


def _proj_kernel(h_ref, w_ref, o_ref, wb_ref, *, gate):
    @pl.when(pl.program_id(1) == 0)
    def _():
        wb_ref[...] = w_ref[...].astype(BF16)

    r = jnp.dot(h_ref[...], wb_ref[...], preferred_element_type=F32)
    if gate:
        r = _sigmoid(r)
    o_ref[...] = r.astype(o_ref.dtype)


def _in_proj(h2d, w_in, col0, ncols, out_dtype, gate):
    n, d = h2d.shape
    tn, tm = 1024, 1024
    jb = col0 // tn
    return pl.pallas_call(
        functools.partial(_proj_kernel, gate=gate),
        out_shape=jax.ShapeDtypeStruct((n, ncols), out_dtype),
        grid=(ncols // tn, n // tm),
        in_specs=[pl.BlockSpec((tm, d), lambda j, i: (i, 0)),
                  pl.BlockSpec((None, d, tn), lambda j, i: (0, 0, j + jb))],
        out_specs=pl.BlockSpec((tm, tn), lambda j, i: (i, j)),
        scratch_shapes=[pltpu.VMEM((d, tn), BF16)],
        compiler_params=_params("arbitrary", "arbitrary"),
        name="in_proj_gate" if gate else "in_proj_qkv",
    )(h2d, w_in)


def _rel_bucket(n):
    n = jnp.maximum(n, 0)
    max_exact = REL_BUCKETS // 2
    nf = jnp.maximum(n, 1).astype(F32)
    large = max_exact + (jnp.log(nf / max_exact) / math.log(REL_MAX_DIST / max_exact)
                         * (REL_BUCKETS - max_exact)).astype(I32)
    large = jnp.minimum(large, REL_BUCKETS - 1)
    return jnp.where(n < max_exact, n, large)


def _bias_rows(rel_table, t):
    assert 2 * t - (t - 1) >= REL_MAX_DIST, "far tiles must sit wholly in the last bucket"
    nb, h = rel_table.shape
    dist = jnp.arange(-(t - 1), 3 * t, dtype=I32)
    by_dist = rel_table[_rel_bucket(dist)].astype(F32).T
    rows = [jnp.pad(by_dist[:, delta * t:delta * t + 2 * t - 1][:, ::-1], ((0, 0), (0, 1)))
            for delta in range(2)]
    rows.append(jnp.broadcast_to(by_dist[:, -1:], (h, 2 * t)))
    return jnp.stack(rows, axis=1)


def _lane_chunks(x):
    return [x[:, c * LANES:(c + 1) * LANES] for c in range(x.shape[1] // LANES)]


def _diff_attn_kernel(lam_ref, g_ref, brow_ref, q_ref, k_ref, v_ref, o_ref, s1_ref, s2_ref, bias_ref,
                      *, t, nsteps, qpt, lam_init):
    step = pl.program_id(2)
    lp = lam_ref[...]
    lam = (jnp.exp(jnp.sum(lp[0:1] * lp[1:2], axis=-1, keepdims=True))
           - jnp.exp(jnp.sum(lp[2:3] * lp[3:4], axis=-1, keepdims=True)) + lam_init)
    lane = lax.broadcasted_iota(I32, (t, HEAD_DIM), 1)

    qrow = lax.broadcasted_iota(I32, (t, t), 0)
    kcol = lax.broadcasted_iota(I32, (t, t), 1)
    for delta in range(2):
        gen = jnp.broadcast_to(brow_ref[delta:delta + 1, :], (t, 2 * t))
        tile = pltpu.roll(gen, t + 1, 1, stride=1, stride_axis=0)[:, :t]
        if delta == 0:
            tile = jnp.where(kcol <= qrow, tile, -jnp.inf)
        bias_ref[delta] = tile
    bias_ref[2] = jnp.broadcast_to(brow_ref[2:3, :t], (t, t))

    def scores(qh, s_ref, nk):
        mx = None
        for j in range(nk):
            s = (lax.dot_general(qh, k_ref[j * t:(j + 1) * t, :], NT_DIMS, preferred_element_type=F32)
                 + bias_ref[min(nk - 1 - j, 2)])
            s_ref[:, j * t:(j + 1) * t] = s
            for ch in _lane_chunks(s):
                mx = ch if mx is None else jnp.maximum(mx, ch)
        return jnp.broadcast_to(jnp.max(mx, axis=-1, keepdims=True), (t, LANES))

    def softmax_pv(s_ref, mb, nk):
        lsum = jnp.zeros((t, LANES), F32)
        acc = jnp.zeros((t, HEAD_DIM), F32)
        for j in range(nk):
            ps = [jnp.exp(ch - mb) for ch in _lane_chunks(s_ref[:, j * t:(j + 1) * t])]
            for p in ps:
                lsum = lsum + p
            acc = acc + jnp.dot(jnp.concatenate(ps, axis=1).astype(BF16), v_ref[j * t:(j + 1) * t, :],
                                preferred_element_type=F32)
        return acc / jnp.sum(lsum, axis=-1, keepdims=True)

    def q_tile(lt, nk):
        rows = slice(lt * t, (lt + 1) * t)
        q = q_ref[rows, :] * jnp.asarray(DA_HALF_DIM ** -0.5, BF16)
        q1 = jnp.where(lane < DA_HALF_DIM, q, jnp.zeros_like(q))
        q2 = jnp.where(lane >= DA_HALF_DIM, q, jnp.zeros_like(q))
        s1, s2 = s1_ref.at[lt % 2], s2_ref.at[lt % 2]
        m1 = scores(q1, s1, nk)
        m2 = scores(q2, s2, nk)
        o = softmax_pv(s1, m1, nk) - lam * softmax_pv(s2, m2, nk)
        ms = jnp.mean(o * o, axis=-1, keepdims=True)
        o = (o * lax.rsqrt(ms + EPS) * g_ref[...]) * (1.0 - lam_init)
        o_ref[rows, :] = o.astype(o_ref.dtype)

    def variant(first_tile):
        for lt in range(qpt):
            q_tile(lt, first_tile + lt + 1)

    for i in range(nsteps):
        pl.when(step == i)(functools.partial(variant, i * qpt))


def _diff_attention(qkv3, lam_params, g_subln, bias, lam_init, t, qpt):
    bsz, s, _ = qkv3.shape
    h = DA_HEADS
    nsteps = s // (t * qpt)
    return pl.pallas_call(
        functools.partial(_diff_attn_kernel, t=t, nsteps=nsteps, qpt=qpt, lam_init=lam_init),
        out_shape=jax.ShapeDtypeStruct((bsz, s, h * HEAD_DIM), BF16),
        scratch_shapes=[pltpu.VMEM((2, t, s), F32), pltpu.VMEM((2, t, s), F32), pltpu.VMEM((3, t, t), F32)],
        grid=(bsz, h, nsteps),
        in_specs=[pl.BlockSpec((4, DA_HALF_DIM), lambda b, hh, i: (0, 0)),
                  pl.BlockSpec((1, HEAD_DIM), lambda b, hh, i: (0, 0)),
                  pl.BlockSpec((None, 3, 2 * t), lambda b, hh, i: (hh, 0, 0)),
                  pl.BlockSpec((None, t * qpt, HEAD_DIM), lambda b, hh, i: (b, i, hh)),
                  pl.BlockSpec((None, s, HEAD_DIM), lambda b, hh, i: (b, 0, h + hh)),
                  pl.BlockSpec((None, s, HEAD_DIM), lambda b, hh, i: (b, 0, 2 * h + hh))],
        out_specs=pl.BlockSpec((None, t * qpt, HEAD_DIM), lambda b, hh, i: (b, i, hh)),
        compiler_params=_params("arbitrary", "arbitrary", "arbitrary"),
        name="diff_attention",
    )(lam_params, g_subln.reshape(1, HEAD_DIM), bias, qkv3, qkv3, qkv3)


def _sb_attn_kernel(q_ref, k_ref, v_ref, o_ref, e_ref, *, t, nsteps, qpt, scale):
    step = pl.program_id(2)
    row = lax.broadcasted_iota(I32, (t, t), 0)
    col = lax.broadcasted_iota(I32, (t, t), 1)
    neg_tri = jnp.where(row >= col, -1.0, 0.0).astype(BF16)
    past = col < row

    def q_tile(lt, nk):
        rows = slice(lt * t, (lt + 1) * t)
        q = q_ref[rows, :]
        e = e_ref.at[lt % 2]
        totals = []
        for j in range(nk):
            z = lax.dot_general(q, k_ref[j * t:(j + 1) * t, :], NT_DIMS, preferred_element_type=F32) * scale
            softplus = jnp.maximum(z, 0.0) + jnp.log(1.0 + jnp.exp2(jnp.abs(z) * (-math.log2(math.e))))
            if j == nk - 1:
                softplus = jnp.where(past, softplus, 0.0)
            incl = jnp.dot(softplus.astype(BF16), neg_tri, preferred_element_type=F32)
            e[:, j * t:(j + 1) * t] = z + incl
            totals.append(incl[:, 0:1])

        acc = jnp.zeros((t, HEAD_DIM), F32)
        later = jnp.zeros((t, 1), F32)
        for j in reversed(range(nk)):
            lb = jnp.broadcast_to(later, (t, LANES))
            ws = [jnp.exp(ch + lb) for ch in _lane_chunks(e[:, j * t:(j + 1) * t])]
            w = jnp.concatenate(ws, axis=1)
            if j == nk - 1:
                w = jnp.where(past, w, 0.0)
            acc = acc + jnp.dot(w.astype(BF16), v_ref[j * t:(j + 1) * t, :], preferred_element_type=F32)
            later = later + totals[j]
        o_ref[rows, :] = acc.astype(o_ref.dtype)

    def variant(first_tile):
        for lt in range(qpt):
            q_tile(lt, first_tile + lt + 1)

    for i in range(nsteps):
        pl.when(step == i)(functools.partial(variant, i * qpt))


def _sb_attention(qkv3, col_block0, t, qpt):
    bsz, s, _ = qkv3.shape
    h = SB_HEADS
    nsteps = s // (t * qpt)
    return pl.pallas_call(
        functools.partial(_sb_attn_kernel, t=t, nsteps=nsteps, qpt=qpt, scale=HEAD_DIM ** -0.5),
        out_shape=jax.ShapeDtypeStruct((bsz, s, h * HEAD_DIM), BF16),
        scratch_shapes=[pltpu.VMEM((2, t, s), F32)],
        grid=(bsz, h, nsteps),
        in_specs=[pl.BlockSpec((None, t * qpt, HEAD_DIM), lambda b, hh, i: (b, i, col_block0 + hh)),
                  pl.BlockSpec((None, s, HEAD_DIM), lambda b, hh, i: (b, 0, col_block0 + h + hh)),
                  pl.BlockSpec((None, s, HEAD_DIM), lambda b, hh, i: (b, 0, col_block0 + 2 * h + hh))],
        out_specs=pl.BlockSpec((None, t * qpt, HEAD_DIM), lambda b, hh, i: (b, i, hh)),
        compiler_params=_params("arbitrary", "arbitrary", "arbitrary"),
        name="sb_attention",
    )(qkv3, qkv3, qkv3)


def _merge_kernel(oa_ref, ob_ref, sa_ref, sb_ref, wa_ref, wb_ref, m_ref):
    merged = (sa_ref[...] * jnp.dot(oa_ref[...], wa_ref[...], preferred_element_type=F32)
              + sb_ref[...] * jnp.dot(ob_ref[...], wb_ref[...], preferred_element_type=F32))
    m_ref[...] = merged.astype(m_ref.dtype)


def _merge_branches(oa, ob, gates, wa, wb):
    n, width = oa.shape
    d = wa.shape[1]
    tm = 512
    const = lambda shape: pl.BlockSpec(shape, lambda i: (0,) * len(shape), pipeline_mode=pl.Buffered(1))
    return pl.pallas_call(
        _merge_kernel,
        out_shape=jax.ShapeDtypeStruct((n, d), BF16),
        grid=(n // tm,),
        in_specs=[pl.BlockSpec((tm, width), lambda i: (i, 0)),
                  pl.BlockSpec((tm, width), lambda i: (i, 0)),
                  pl.BlockSpec((tm, d), lambda i: (i, 0)),
                  pl.BlockSpec((tm, d), lambda i: (i, 1)),
                  const((width, d)), const((width, d))],
        out_specs=pl.BlockSpec((tm, d), lambda i: (i, 0)),
        compiler_params=_params("arbitrary"),
        name="merge_branches",
    )(oa, ob, gates, gates, wa, wb)


def _post_kernel(m_ref, x_ref, gm_ref, scf_ref, shf_ref, gffn_ref,
                 wo_ref, wrh_ref, wrl_ref, br_ref,
                 x1_ref, h2_ref, lg_ref):
    y = jnp.dot(m_ref[...], wo_ref[...], preferred_element_type=F32)
    x1 = x_ref[...] + gm_ref[...] * y
    x1_ref[...] = x1
    ms = jnp.mean(x1 * x1, axis=-1, keepdims=True)
    h2 = (x1 * lax.rsqrt(ms + EPS) * gffn_ref[...]) * (1.0 + scf_ref[...]) + shf_ref[...]
    h2_ref[...] = h2
    hb = h2.astype(BF16)
    hl = (h2 - hb.astype(F32)).astype(BF16)
    wrh = wrh_ref[...]
    lg = (lax.dot_general(wrh, hb, NT_DIMS, preferred_element_type=F32)
          + lax.dot_general(wrh, hl, NT_DIMS, preferred_element_type=F32)
          + lax.dot_general(wrl_ref[...], hb, NT_DIMS, preferred_element_type=F32))
    lg_ref[...] = lg + br_ref[...]


def _post_attention(merged, x2d, mod3, g_ffn, wo, wrh, wrl, br, seq):
    n, d = x2d.shape
    tm = 512
    per_b = seq // tm
    const = lambda shape: pl.BlockSpec(shape, lambda i: (0,) * len(shape), pipeline_mode=pl.Buffered(1))
    modspec = lambda idx: pl.BlockSpec((None, 1, d), lambda i: (i // per_b, 0, idx))
    return pl.pallas_call(
        _post_kernel,
        out_shape=(jax.ShapeDtypeStruct((n, d), F32),
                   jax.ShapeDtypeStruct((n, d), F32),
                   jax.ShapeDtypeStruct((LANES, n), F32)),
        grid=(n // tm,),
        in_specs=[pl.BlockSpec((tm, d), lambda i: (i, 0)),
                  pl.BlockSpec((tm, d), lambda i: (i, 0)),
                  modspec(2), modspec(4), modspec(3),
                  const((1, d)),
                  const((d, d)),
                  const((LANES, d)), const((LANES, d)), const((LANES, 1))],
        out_specs=(pl.BlockSpec((tm, d), lambda i: (i, 0)),
                   pl.BlockSpec((tm, d), lambda i: (i, 0)),
                   pl.BlockSpec((LANES, tm), lambda i: (0, i))),
        compiler_params=_params("arbitrary"),
        name="post_attention",
    )(merged, x2d, mod3, mod3, mod3, g_ffn.reshape(1, d), wo, wrh, wrl, br)


def _first_index_of_max(vals, iota, nrows):
    mx = jnp.max(vals, axis=0, keepdims=True)
    idx = jnp.min(jnp.where(vals == mx, iota, nrows), axis=0, keepdims=True)
    return mx, idx


def _route_kernel(lg_ref, e_ref, w_ref):
    g = N_GROUPS
    epg = EXPERTS_PER_GROUP
    lg = lg_ref[...]
    gl = lg[0:g, :]
    iota = lax.broadcasted_iota(I32, gl.shape, 0)
    gmax, gidx = _first_index_of_max(gl, iota, g)
    p_g = 1.0 / jnp.sum(jnp.exp(gl - gmax), axis=0, keepdims=True)

    esel = jnp.zeros((epg, lg.shape[1]), F32)
    for gi in range(g):
        esel = jnp.where(gidx == gi, lg[g + gi * epg:g + (gi + 1) * epg, :], esel)
    emax = jnp.max(esel, axis=0, keepdims=True)
    ex = jnp.exp(esel - emax)
    prob = ex / jnp.sum(ex, axis=0, keepdims=True)

    p0, i0 = _first_index_of_max(prob, iota, epg)
    rest = jnp.where(iota == i0, -1.0, prob)
    p1, i1 = _first_index_of_max(rest, iota, epg)
    tot = p0 + p1
    e_ref[0:1, :] = gidx * epg + i0
    e_ref[1:2, :] = gidx * epg + i1
    w_ref[0:1, :] = p_g * (p0 / tot)
    w_ref[1:2, :] = p_g * (p1 / tot)


def _route(logits_t):
    rows, n = logits_t.shape
    tn = 1024
    return pl.pallas_call(
        _route_kernel,
        out_shape=(jax.ShapeDtypeStruct((2, n), I32), jax.ShapeDtypeStruct((2, n), F32)),
        grid=(n // tn,),
        in_specs=[pl.BlockSpec((rows, tn), lambda i: (0, i))],
        out_specs=(pl.BlockSpec((2, tn), lambda i: (0, i)), pl.BlockSpec((2, tn), lambda i: (0, i))),
        compiler_params=_params("arbitrary"),
        name="route",
    )(logits_t)


VISIT_ROWS = 512
MOE_TILE = 256
MOE_FCHUNK = 512
GATHER_CHUNK = 32


def _moe_kernel(ve_ref, vnt_ref, vcnt_ref,
                tokc_ref, tokn_ref, h_hbm, wg_ref, wu_ref, wd_ref,
                y_ref,
                xbuf, sem, wgb, wub, wdb, *, nf, nv):
    v = pl.program_id(0)
    f = pl.program_id(1)
    slot = v % 2
    nt = vnt_ref[v]
    chunks_per_step = VISIT_ROWS // nf // GATHER_CHUNK

    def n_chunks(vv):
        return lax.shift_right_logical(vcnt_ref[vv] + (GATHER_CHUNK - 1), GATHER_CHUNK.bit_length() - 1)

    def row_copy(tok_ref, dst_slot, r):
        tok = tok_ref[0, r]
        return pltpu.make_async_copy(h_hbm.at[pl.ds(tok, 1)], xbuf.at[dst_slot, pl.ds(r, 1)],
                                     sem.at[dst_slot])

    def issue_chunks(tok_ref, dst_slot, c0, n):
        def chunk(c, carry):
            base = (c0 + c) * GATHER_CHUNK

            def body(r, cc):
                row_copy(tok_ref, dst_slot, base + r).start()
                return cc
            lax.fori_loop(0, GATHER_CHUNK, body, 0, unroll=8)
            return carry
        lax.fori_loop(0, n, chunk, 0)

    def wait_chunks(dst_slot, n):
        def chunk(c, carry):
            pltpu.make_async_copy(h_hbm.at[pl.ds(0, GATHER_CHUNK)],
                                  xbuf.at[dst_slot, pl.ds(0, GATHER_CHUNK)], sem.at[dst_slot]).wait()
            return carry
        lax.fori_loop(0, n, chunk, 0)

    @pl.when(jnp.logical_and(v == 0, f == 0))
    def _():
        xbuf[...] = jnp.zeros_like(xbuf)
        issue_chunks(tokc_ref, 0, 0, n_chunks(0))

    @pl.when(jnp.logical_and(f == 0, nt > 0))
    def _():
        wait_chunks(slot, n_chunks(v))

    @pl.when(f == 0)
    def _():
        y_ref[...] = jnp.zeros_like(y_ref)

    nxt = jnp.minimum(v + 1, nv - 1)
    chunks_next = jnp.where(v + 1 < nv, n_chunks(nxt), 0)
    c0 = f * chunks_per_step
    issue_chunks(tokn_ref, 1 - slot, c0, jnp.clip(chunks_next - c0, 0, chunks_per_step))

    @pl.when(nt > 0)
    def _():
        wgb[...] = wg_ref[...].astype(BF16)
        wub[...] = wu_ref[...].astype(BF16)
        wdb[...] = wd_ref[...].astype(BF16)
        for tl in range(VISIT_ROWS // MOE_TILE):
            @pl.when(tl < nt)
            def _():
                rows = pl.ds(tl * MOE_TILE, MOE_TILE)
                x = xbuf[slot, rows, :].astype(BF16)
                a = jnp.dot(x, wgb[...], preferred_element_type=F32)
                u = jnp.dot(x, wub[...], preferred_element_type=F32)
                hmid = ((a * _sigmoid(a)) * u).astype(BF16)
                y_ref[rows, :] += jnp.dot(hmid, wdb[...], preferred_element_type=F32)


def _moe_experts(h2, row_tok, vis_e, vis_nt, vis_cnt, w_gate, w_up, w_down):
    n, d = h2.shape
    n_exp, _, dexp = w_gate.shape
    nv = vis_e.shape[0]
    nf = dexp // MOE_FCHUNK
    assert GATHER_CHUNK & (GATHER_CHUNK - 1) == 0 and VISIT_ROWS % (nf * GATHER_CHUNK) == 0
    tok3 = row_tok.reshape(nv, 1, VISIT_ROWS)

    def fidx(v, f, vnt):
        return jnp.where(vnt[v] > 0, f, nf - 1)

    grid_spec = pltpu.PrefetchScalarGridSpec(
        num_scalar_prefetch=3,
        grid=(nv, nf),
        in_specs=[
            pl.BlockSpec((None, 1, VISIT_ROWS), lambda v, f, ve, vnt, vcnt: (v, 0, 0),
                         memory_space=pltpu.SMEM),
            pl.BlockSpec((None, 1, VISIT_ROWS), lambda v, f, ve, vnt, vcnt: (jnp.minimum(v + 1, nv - 1), 0, 0),
                         memory_space=pltpu.SMEM),
            pl.BlockSpec(memory_space=pl.ANY),
            pl.BlockSpec((None, d, MOE_FCHUNK), lambda v, f, ve, vnt, vcnt: (ve[v], 0, fidx(v, f, vnt))),
            pl.BlockSpec((None, d, MOE_FCHUNK), lambda v, f, ve, vnt, vcnt: (ve[v], 0, fidx(v, f, vnt))),
            pl.BlockSpec((None, MOE_FCHUNK, d), lambda v, f, ve, vnt, vcnt: (ve[v], fidx(v, f, vnt), 0)),
        ],
        out_specs=pl.BlockSpec((VISIT_ROWS, d), lambda v, f, ve, vnt, vcnt: (v, 0)),
        scratch_shapes=[pltpu.VMEM((2, VISIT_ROWS, d), F32),
                        pltpu.SemaphoreType.DMA((2,)),
                        pltpu.VMEM((d, MOE_FCHUNK), BF16),
                        pltpu.VMEM((d, MOE_FCHUNK), BF16),
                        pltpu.VMEM((MOE_FCHUNK, d), BF16)],
    )
    return pl.pallas_call(
        functools.partial(_moe_kernel, nf=nf, nv=nv),
        out_shape=jax.ShapeDtypeStruct((nv * VISIT_ROWS, d), F32),
        grid_spec=grid_spec,
        compiler_params=_params("arbitrary", "arbitrary"),
        name="moe_experts",
    )(vis_e, vis_nt, vis_cnt, tok3, tok3, h2, w_gate, w_up, w_down)


RANK_BLOCK = 256


def _slot_kernel(e_ref, dest_ref, cnt_ref, rank_ref, *, n_exp):
    nk, n = e_ref.shape
    eids = lax.broadcasted_iota(I32, (n_exp, RANK_BLOCK), 0)
    row = lax.broadcasted_iota(I32, (RANK_BLOCK, RANK_BLOCK), 0)
    col = lax.broadcasted_iota(I32, (RANK_BLOCK, RANK_BLOCK), 1)
    before = (row < col).astype(BF16)
    carry = jnp.zeros((n_exp, 1), F32)
    for k in range(nk):
        for c0 in range(0, n, RANK_BLOCK):
            hit = eids == e_ref[k:k + 1, c0:c0 + RANK_BLOCK]
            onehot = jnp.where(hit, 1.0, 0.0)
            earlier = jnp.dot(onehot.astype(BF16), before, preferred_element_type=F32) + carry
            rank = jnp.sum(jnp.where(hit, earlier, 0.0), axis=0, keepdims=True)
            rank_ref[k:k + 1, c0:c0 + RANK_BLOCK] = rank.astype(I32)
            carry = carry + jnp.sum(onehot, axis=1, keepdims=True)
    cnt_ref[...] = jnp.broadcast_to(carry, cnt_ref.shape).astype(I32)

    n_vis = lax.shift_right_logical(carry.astype(I32) + (VISIT_ROWS - 1),
                                    VISIT_ROWS.bit_length() - 1).astype(F32)
    ei = lax.broadcasted_iota(I32, (n_exp, n_exp), 0)
    ej = lax.broadcasted_iota(I32, (n_exp, n_exp), 1)
    lower = (ej < ei).astype(BF16)
    first_row = jnp.dot(lower, jnp.broadcast_to(n_vis, (n_exp, LANES)).astype(BF16),
                        preferred_element_type=F32)[:, 0:1] * float(VISIT_ROWS)
    for k in range(nk):
        for c0 in range(0, n, RANK_BLOCK):
            hit = eids == e_ref[k:k + 1, c0:c0 + RANK_BLOCK]
            base = jnp.sum(jnp.where(hit, first_row, 0.0), axis=0, keepdims=True)
            dest_ref[k:k + 1, c0:c0 + RANK_BLOCK] = rank_ref[k:k + 1, c0:c0 + RANK_BLOCK] + base.astype(I32)


def _assignment_slots(e2, n_exp):
    nk, n = e2.shape
    assert (n // VISIT_ROWS * nk + n_exp) <= 256, "visit counts must stay exact in bf16"
    return pl.pallas_call(
        functools.partial(_slot_kernel, n_exp=n_exp),
        out_shape=(jax.ShapeDtypeStruct((nk, n), I32), jax.ShapeDtypeStruct((n_exp, LANES), I32)),
        scratch_shapes=[pltpu.VMEM((nk, n), I32)],
        compiler_params=pltpu.CompilerParams(vmem_limit_bytes=VMEM_LIMIT_BYTES),
        name="assignment_slots",
    )(e2)


def _invert_kernel(dest_ref, tok_ref, zeros_vmem, sem):
    nk, n = dest_ref.shape
    zeros_vmem[...] = jnp.zeros_like(zeros_vmem)
    fill = pltpu.make_async_copy(zeros_vmem, tok_ref, sem)
    fill.start()
    fill.wait()

    def body(t, c):
        for k in range(nk):
            tok_ref[dest_ref[k, t]] = t
        return c
    lax.fori_loop(0, n, body, 0, unroll=8)


def _slot_tokens(dest2, n_slots):
    return pl.pallas_call(
        _invert_kernel,
        out_shape=jax.ShapeDtypeStruct((n_slots,), I32),
        in_specs=[pl.BlockSpec(memory_space=pltpu.SMEM)],
        out_specs=pl.BlockSpec(memory_space=pltpu.SMEM),
        scratch_shapes=[pltpu.VMEM((n_slots,), I32), pltpu.SemaphoreType.DMA],
        name="slot_tokens",
    )(dest2)


def _expert_layout(e2, n_exp):
    n = e2.shape[1]
    m = 2 * n
    nv = m // VISIT_ROWS + n_exp
    dest2, cnt = _assignment_slots(e2, n_exp)
    counts = cnt[:, 0]
    n_vis = (counts + VISIT_ROWS - 1) // VISIT_ROWS
    cum_vis = jnp.cumsum(n_vis)
    vbase = cum_vis - n_vis
    row_tok = _slot_tokens(dest2, nv * VISIT_ROWS)

    n_used = cum_vis[-1]
    vid = jnp.arange(nv, dtype=I32)
    used = vid < n_used
    ve = jnp.minimum(jnp.sum((cum_vis[None, :] <= vid[:, None]).astype(I32), axis=1), n_exp - 1)
    rem = counts[ve] - (vid - vbase[ve]) * VISIT_ROWS
    nt = jnp.clip((rem + MOE_TILE - 1) // MOE_TILE, 0, VISIT_ROWS // MOE_TILE)
    last = jnp.maximum(n_used - 1, 0)
    vis_nt = jnp.where(used, nt, 0).astype(I32)
    vis_cnt = jnp.where(used, jnp.clip(rem, 0, VISIT_ROWS), 0).astype(I32)
    vis_e = jnp.where(used, ve, ve[last]).astype(I32)
    return dest2, row_tok, vis_e, vis_nt, vis_cnt


def _final_kernel(posc_ref, posn_ref, x1_ref, gf_ref, w_ref, g_ref, ys_hbm, o_ref, ybuf, sem,
                  *, tm, nsteps):
    i = pl.program_id(0)
    slot = i % 2

    def row_copy(pos_ref, dst_slot, j):
        return pltpu.make_async_copy(ys_hbm.at[pl.ds(pos_ref[0, j], 1)],
                                     ybuf.at[dst_slot, pl.ds(j, 1)], sem.at[dst_slot])

    def issue(pos_ref, dst_slot):
        def body(j, c):
            row_copy(pos_ref, dst_slot, j).start()
            return c
        lax.fori_loop(0, 2 * tm, body, 0, unroll=8)

    @pl.when(i == 0)
    def _():
        issue(posc_ref, 0)

    @pl.when(i + 1 < nsteps)
    def _():
        issue(posn_ref, 1 - slot)

    pltpu.make_async_copy(ys_hbm.at[pl.ds(0, 2 * tm)], ybuf.at[slot], sem.at[slot]).wait()

    w = w_ref[...]
    moe = w[:, 0:1] * ybuf[slot, 0:tm, :] + w[:, 1:2] * ybuf[slot, tm:2 * tm, :]
    x = x1_ref[...] + gf_ref[...] * moe
    ms = jnp.mean(x * x, axis=-1, keepdims=True)
    o_ref[...] = x * lax.rsqrt(ms + EPS) * g_ref[...]


def _combine_final(x1, ys, pos, gate_w, mod3, g_final, seq):
    n, d = x1.shape
    tm = 256
    nsteps = n // tm
    per_b = seq // tm
    pos3 = pos.reshape(2, nsteps, tm).transpose(1, 0, 2).reshape(nsteps, 1, 2 * tm)
    return pl.pallas_call(
        functools.partial(_final_kernel, tm=tm, nsteps=nsteps),
        out_shape=jax.ShapeDtypeStruct((n, d), F32),
        grid=(nsteps,),
        in_specs=[pl.BlockSpec((None, 1, 2 * tm), lambda i: (i, 0, 0), memory_space=pltpu.SMEM),
                  pl.BlockSpec((None, 1, 2 * tm), lambda i: (jnp.minimum(i + 1, nsteps - 1), 0, 0),
                               memory_space=pltpu.SMEM),
                  pl.BlockSpec((tm, d), lambda i: (i, 0)),
                  pl.BlockSpec((None, 1, d), lambda i: (i // per_b, 0, 5)),
                  pl.BlockSpec((tm, 2), lambda i: (i, 0)),
                  pl.BlockSpec((1, d), lambda i: (0, 0)),
                  pl.BlockSpec(memory_space=pl.ANY)],
        out_specs=pl.BlockSpec((tm, d), lambda i: (i, 0)),
        scratch_shapes=[pltpu.VMEM((2, 2 * tm, d), F32), pltpu.SemaphoreType.DMA((2,))],
        compiler_params=_params("arbitrary"),
        name="combine_final",
    )(pos3, pos3, x1, mod3, gate_w, g_final.reshape(1, d), ys)


def kernel(x, c, rel_bias_table, w_ada, b_ada, g_mix, w_in, lambda_q1, lambda_k1, lambda_q2,
           lambda_k2, g_subln, w_proj_a, w_proj_b, w_out, g_ffn, w_router_group, b_router_group,
           w_router_expert, b_router_expert, w_expert_gate, w_expert_up, w_expert_down, g_final):
    bsz, seq, d = x.shape
    n = bsz * seq
    depth = w_in.shape[0]
    assert depth == 1, "the MoE combine is fused with the final RMSNorm: one layer only"
    da_width = DA_HEADS * HEAD_DIM
    sb_width = SB_HEADS * HEAD_DIM
    qkv_cols = 3 * da_width + 3 * sb_width
    attn_tile = 256
    attn_qpt = seq // attn_tile
    n_exp = w_expert_gate.shape[1]
    xf = x.reshape(n, d)

    for l in range(depth):
        lam_init = 0.8 - 0.6 * math.exp(-0.3 * l)
        mod = _adaln_mod(c, w_ada[l:l + 1], b_ada[l])
        mod3 = mod.reshape(bsz, 1, N_MOD * d)

        h = _norm_modulate(xf.reshape(bsz, seq, d), g_mix[l], mod3, 1, 0).reshape(n, d)
        qkv = _in_proj(h, w_in[l:l + 1], 0, qkv_cols, BF16, gate=False)
        gates = _in_proj(h, w_in[l:l + 1], qkv_cols, 2 * d, F32, gate=True)
        qkv3 = qkv.reshape(bsz, seq, qkv_cols)

        lam_params = jnp.stack([lambda_q1[l], lambda_k1[l], lambda_q2[l], lambda_k2[l]]).astype(F32)
        bias = _bias_rows(rel_bias_table, attn_tile)
        oa = _diff_attention(qkv3, lam_params, g_subln[l], bias, lam_init, attn_tile, attn_qpt)
        ob = _sb_attention(qkv3, 3 * DA_HEADS, attn_tile, attn_qpt)

        w_r = jnp.concatenate([w_router_group[l], w_router_expert[l]], axis=1).astype(F32).T
        w_r = jnp.pad(w_r, ((0, LANES - w_r.shape[0]), (0, 0)))
        wrh = w_r.astype(BF16)
        wrl = (w_r - wrh.astype(F32)).astype(BF16)
        b_r = jnp.concatenate([b_router_group[l], b_router_expert[l]]).astype(F32)
        b_r = jnp.pad(b_r, (0, LANES - b_r.shape[0])).reshape(LANES, 1)
        merged = _merge_branches(oa.reshape(n, da_width), ob.reshape(n, sb_width), gates,
                                 w_proj_a[l].astype(BF16), w_proj_b[l].astype(BF16))
        x1, h2, logits_t = _post_attention(merged, xf, mod3, g_ffn[l], w_out[l].astype(BF16),
                                           wrh, wrl, b_r, seq)

        e2, g2 = _route(logits_t)
        pos, row_tok, vis_e, vis_nt, vis_cnt = _expert_layout(e2, n_exp)
        ys = _moe_experts(h2, row_tok, vis_e, vis_nt, vis_cnt,
                          w_expert_gate[l], w_expert_up[l], w_expert_down[l])
        out = _combine_final(x1, ys, pos, g2.T, mod3, g_final, seq)
    return out.reshape(bsz, seq, d)
```

```python
import functools
import math

import jax
import jax.numpy as jnp
from jax import lax
from jax.experimental import pallas as pl
from jax.experimental.pallas import tpu as pltpu

F32 = jnp.float32
BF16 = jnp.bfloat16
I32 = jnp.int32
EPS = 1e-6

DA_HEADS = 8
DA_HALF_DIM = 64
SB_HEADS = 8
HEAD_DIM = 128
REL_BUCKETS = 32
REL_MAX_DIST = 128
N_GROUPS = 8
EXPERTS_PER_GROUP = 8
N_MOD = 6

VMEM_LIMIT_BYTES = 56 * 1024 * 1024
LANES = 128

NT_DIMS = (((1,), (1,)), ((), ()))


def _params(*sem):
    return pltpu.CompilerParams(dimension_semantics=sem, vmem_limit_bytes=VMEM_LIMIT_BYTES)


def _sigmoid(v):
    return 1.0 / (1.0 + jnp.exp(-v))


def _mod_kernel(c_ref, w_ref, b_ref, o_ref):
    c = c_ref[...]
    s = (c * _sigmoid(c)).astype(BF16)
    o_ref[...] = jnp.dot(s, w_ref[...].astype(BF16), preferred_element_type=F32) + b_ref[...]


def _adaln_mod(c, w_ada, b_ada):
    bsz, d = c.shape
    ncol = w_ada.shape[-1]
    tn = 1024
    return pl.pallas_call(
        _mod_kernel,
        out_shape=jax.ShapeDtypeStruct((bsz, ncol), F32),
        grid=(ncol // tn,),
        in_specs=[pl.BlockSpec((bsz, d), lambda j: (0, 0)),
                  pl.BlockSpec((None, d, tn), lambda j: (0, 0, j)),
                  pl.BlockSpec((1, tn), lambda j: (0, j))],
        out_specs=pl.BlockSpec((bsz, tn), lambda j: (0, j)),
        compiler_params=_params("arbitrary"),
        name="adaln_mod",
    )(c, w_ada, b_ada.reshape(1, ncol))


def _hnorm_kernel(x_ref, g_ref, sc_ref, sh_ref, o_ref):
    x = x_ref[...]
    ms = jnp.mean(x * x, axis=-1, keepdims=True)
    y = x * lax.rsqrt(ms + EPS) * g_ref[...]
    o_ref[...] = (y * (1.0 + sc_ref[...]) + sh_ref[...]).astype(o_ref.dtype)


def _norm_modulate(x, g, mod3, scale_idx, shift_idx):
    bsz, s, d = x.shape
    ts = 512
    return pl.pallas_call(
        _hnorm_kernel,
        out_shape=jax.ShapeDtypeStruct((bsz, s, d), BF16),
        grid=(bsz, s // ts),
        in_specs=[pl.BlockSpec((None, ts, d), lambda b, i: (b, i, 0)),
                  pl.BlockSpec((1, d), lambda b, i: (0, 0)),
                  pl.BlockSpec((None, 1, d), lambda b, i: (b, 0, scale_idx)),
                  pl.BlockSpec((None, 1, d), lambda b, i: (b, 0, shift_idx))],
        out_specs=pl.BlockSpec((None, ts, d), lambda b, i: (b, i, 0)),
        compiler_params=_params("arbitrary", "arbitrary"),
        name="norm_modulate",
    )(x, g.reshape(1, d), mod3, mod3)


def _proj_kernel(h_ref, w_ref, o_ref, wb_ref, *, gate):
    @pl.when(pl.program_id(1) == 0)
    def _():
        wb_ref[...] = w_ref[...].astype(BF16)

    r = jnp.dot(h_ref[...], wb_ref[...], preferred_element_type=F32)
    if gate:
        r = _sigmoid(r)
    o_ref[...] = r.astype(o_ref.dtype)


def _in_proj(h2d, w_in, col0, ncols, out_dtype, gate):
    n, d = h2d.shape
    tn, tm = 1024, 1024
    jb = col0 // tn
    return pl.pallas_call(
        functools.partial(_proj_kernel, gate=gate),
        out_shape=jax.ShapeDtypeStruct((n, ncols), out_dtype),
        grid=(ncols // tn, n // tm),
        in_specs=[pl.BlockSpec((tm, d), lambda j, i: (i, 0)),
                  pl.BlockSpec((None, d, tn), lambda j, i: (0, 0, j + jb))],
        out_specs=pl.BlockSpec((tm, tn), lambda j, i: (i, j)),
        scratch_shapes=[pltpu.VMEM((d, tn), BF16)],
        compiler_params=_params("arbitrary", "arbitrary"),
        name="in_proj_gate" if gate else "in_proj_qkv",
    )(h2d, w_in)


def _rel_bucket(n):
    n = jnp.maximum(n, 0)
    max_exact = REL_BUCKETS // 2
    nf = jnp.maximum(n, 1).astype(F32)
    large = max_exact + (jnp.log(nf / max_exact) / math.log(REL_MAX_DIST / max_exact)
                         * (REL_BUCKETS - max_exact)).astype(I32)
    large = jnp.minimum(large, REL_BUCKETS - 1)
    return jnp.where(n < max_exact, n, large)


def _bias_rows(rel_table, t):
    assert 2 * t - (t - 1) >= REL_MAX_DIST, "far tiles must sit wholly in the last bucket"
    nb, h = rel_table.shape
    dist = jnp.arange(-(t - 1), 3 * t, dtype=I32)
    by_dist = rel_table[_rel_bucket(dist)].astype(F32).T
    rows = [jnp.pad(by_dist[:, delta * t:delta * t + 2 * t - 1][:, ::-1], ((0, 0), (0, 1)))
            for delta in range(2)]
    rows.append(jnp.broadcast_to(by_dist[:, -1:], (h, 2 * t)))
    return jnp.stack(rows, axis=1)


def _lane_chunks(x):
    return [x[:, c * LANES:(c + 1) * LANES] for c in range(x.shape[1] // LANES)]


def _diff_attn_kernel(lam_ref, g_ref, brow_ref, q_ref, k_ref, v_ref, o_ref, s1_ref, s2_ref, bias_ref,
                      *, t, nsteps, qpt, lam_init):
    step = pl.program_id(2)
    lp = lam_ref[...]
    lam = (jnp.exp(jnp.sum(lp[0:1] * lp[1:2], axis=-1, keepdims=True))
           - jnp.exp(jnp.sum(lp[2:3] * lp[3:4], axis=-1, keepdims=True)) + lam_init)
    lane = lax.broadcasted_iota(I32, (t, HEAD_DIM), 1)

    qrow = lax.broadcasted_iota(I32, (t, t), 0)
    kcol = lax.broadcasted_iota(I32, (t, t), 1)
    for delta in range(2):
        gen = jnp.broadcast_to(brow_ref[delta:delta + 1, :], (t, 2 * t))
        tile = pltpu.roll(gen, t + 1, 1, stride=1, stride_axis=0)[:, :t]
        if delta == 0:
            tile = jnp.where(kcol <= qrow, tile, -jnp.inf)
        bias_ref[delta] = tile
    bias_ref[2] = jnp.broadcast_to(brow_ref[2:3, :t], (t, t))

    def scores(qh, s_ref, nk):
        mx = None
        for j in range(nk):
            s = (lax.dot_general(qh, k_ref[j * t:(j + 1) * t, :], NT_DIMS, preferred_element_type=F32)
                 + bias_ref[min(nk - 1 - j, 2)])
            s_ref[:, j * t:(j + 1) * t] = s
            for ch in _lane_chunks(s):
                mx = ch if mx is None else jnp.maximum(mx, ch)
        return jnp.broadcast_to(jnp.max(mx, axis=-1, keepdims=True), (t, LANES))

    def softmax_pv(s_ref, mb, nk):
        lsum = jnp.zeros((t, LANES), F32)
        acc = jnp.zeros((t, HEAD_DIM), F32)
        for j in range(nk):
            ps = [jnp.exp(ch - mb) for ch in _lane_chunks(s_ref[:, j * t:(j + 1) * t])]
            for p in ps:
                lsum = lsum + p
            acc = acc + jnp.dot(jnp.concatenate(ps, axis=1).astype(BF16), v_ref[j * t:(j + 1) * t, :],
                                preferred_element_type=F32)
        return acc / jnp.sum(lsum, axis=-1, keepdims=True)

    def q_tile(lt, nk):
        rows = slice(lt * t, (lt + 1) * t)
        q = q_ref[rows, :] * jnp.asarray(DA_HALF_DIM ** -0.5, BF16)
        q1 = jnp.where(lane < DA_HALF_DIM, q, jnp.zeros_like(q))
        q2 = jnp.where(lane >= DA_HALF_DIM, q, jnp.zeros_like(q))
        s1, s2 = s1_ref.at[lt % 2], s2_ref.at[lt % 2]
        m1 = scores(q1, s1, nk)
        m2 = scores(q2, s2, nk)
        o = softmax_pv(s1, m1, nk) - lam * softmax_pv(s2, m2, nk)
        ms = jnp.mean(o * o, axis=-1, keepdims=True)
        o = (o * lax.rsqrt(ms + EPS) * g_ref[...]) * (1.0 - lam_init)
        o_ref[rows, :] = o.astype(o_ref.dtype)

    def variant(first_tile):
        for lt in range(qpt):
            q_tile(lt, first_tile + lt + 1)

    for i in range(nsteps):
        pl.when(step == i)(functools.partial(variant, i * qpt))


def _diff_attention(qkv3, lam_params, g_subln, bias, lam_init, t, qpt):
    bsz, s, _ = qkv3.shape
    h = DA_HEADS
    nsteps = s // (t * qpt)
    return pl.pallas_call(
        functools.partial(_diff_attn_kernel, t=t, nsteps=nsteps, qpt=qpt, lam_init=lam_init),
        out_shape=jax.ShapeDtypeStruct((bsz, s, h * HEAD_DIM), BF16),
        scratch_shapes=[pltpu.VMEM((2, t, s), F32), pltpu.VMEM((2, t, s), F32), pltpu.VMEM((3, t, t), F32)],
        grid=(bsz, h, nsteps),
        in_specs=[pl.BlockSpec((4, DA_HALF_DIM), lambda b, hh, i: (0, 0)),
                  pl.BlockSpec((1, HEAD_DIM), lambda b, hh, i: (0, 0)),
                  pl.BlockSpec((None, 3, 2 * t), lambda b, hh, i: (hh, 0, 0)),
                  pl.BlockSpec((None, t * qpt, HEAD_DIM), lambda b, hh, i: (b, i, hh)),
                  pl.BlockSpec((None, s, HEAD_DIM), lambda b, hh, i: (b, 0, h + hh)),
                  pl.BlockSpec((None, s, HEAD_DIM), lambda b, hh, i: (b, 0, 2 * h + hh))],
        out_specs=pl.BlockSpec((None, t * qpt, HEAD_DIM), lambda b, hh, i: (b, i, hh)),
        compiler_params=_params("arbitrary", "arbitrary", "arbitrary"),
        name="diff_attention",
    )(lam_params, g_subln.reshape(1, HEAD_DIM), bias, qkv3, qkv3, qkv3)


def _sb_attn_kernel(q_ref, k_ref, v_ref, o_ref, e_ref, *, t, nsteps, qpt, scale):
    step = pl.program_id(2)
    row = lax.broadcasted_iota(I32, (t, t), 0)
    col = lax.broadcasted_iota(I32, (t, t), 1)
    neg_tri = jnp.where(row > col, -1.0, 0.0).astype(BF16)
    past = col < row

    def q_tile(lt, nk):
        rows = slice(lt * t, (lt + 1) * t)
        q = q_ref[rows, :]
        e = e_ref.at[lt % 2]
        totals = []
        for j in range(nk):
            z = lax.dot_general(q, k_ref[j * t:(j + 1) * t, :], NT_DIMS, preferred_element_type=F32) * scale
            softplus = jnp.maximum(z, 0.0) + jnp.log(1.0 + jnp.exp2(jnp.abs(z) * (-math.log2(math.e))))
            log_beta = z - softplus
            if j == nk - 1:
                softplus = jnp.where(past, softplus, 0.0)
            excl = jnp.dot(softplus.astype(BF16), neg_tri, preferred_element_type=F32)
            e[:, j * t:(j + 1) * t] = log_beta + excl
            totals.append(excl[:, 0:1] - softplus[:, 0:1])

        acc = jnp.zeros((t, HEAD_DIM), F32)
        later = jnp.zeros((t, 1), F32)
        for j in reversed(range(nk)):
            lb = jnp.broadcast_to(later, (t, LANES))
            ws = [jnp.exp(ch + lb) for ch in _lane_chunks(e[:, j * t:(j + 1) * t])]
            w = jnp.concatenate(ws, axis=1)
            if j == nk - 1:
                w = jnp.where(past, w, 0.0)
            acc = acc + jnp.dot(w.astype(BF16), v_ref[j * t:(j + 1) * t, :], preferred_element_type=F32)
            later = later + totals[j]
        o_ref[rows, :] = acc.astype(o_ref.dtype)

    def variant(first_tile):
        for lt in range(qpt):
            q_tile(lt, first_tile + lt + 1)

    for i in range(nsteps):
        pl.when(step == i)(functools.partial(variant, i * qpt))


def _sb_attention(qkv3, col_block0, t, qpt):
    bsz, s, _ = qkv3.shape
    h = SB_HEADS
    nsteps = s // (t * qpt)
    return pl.pallas_call(
        functools.partial(_sb_attn_kernel, t=t, nsteps=nsteps, qpt=qpt, scale=HEAD_DIM ** -0.5),
        out_shape=jax.ShapeDtypeStruct((bsz, s, h * HEAD_DIM), BF16),
        scratch_shapes=[pltpu.VMEM((2, t, s), F32)],
        grid=(bsz, h, nsteps),
        in_specs=[pl.BlockSpec((None, t * qpt, HEAD_DIM), lambda b, hh, i: (b, i, col_block0 + hh)),
                  pl.BlockSpec((None, s, HEAD_DIM), lambda b, hh, i: (b, 0, col_block0 + h + hh)),
                  pl.BlockSpec((None, s, HEAD_DIM), lambda b, hh, i: (b, 0, col_block0 + 2 * h + hh))],
        out_specs=pl.BlockSpec((None, t * qpt, HEAD_DIM), lambda b, hh, i: (b, i, hh)),
        compiler_params=_params("arbitrary", "arbitrary", "arbitrary"),
        name="sb_attention",
    )(qkv3, qkv3, qkv3)


def _merge_kernel(oa_ref, ob_ref, sa_ref, sb_ref, wa_ref, wb_ref, m_ref):
    merged = (sa_ref[...] * jnp.dot(oa_ref[...], wa_ref[...], preferred_element_type=F32)
              + sb_ref[...] * jnp.dot(ob_ref[...], wb_ref[...], preferred_element_type=F32))
    m_ref[...] = merged.astype(m_ref.dtype)


def _merge_branches(oa, ob, gates, wa, wb):
    n, width = oa.shape
    d = wa.shape[1]
    tm = 512
    const = lambda shape: pl.BlockSpec(shape, lambda i: (0,) * len(shape), pipeline_mode=pl.Buffered(1))
    return pl.pallas_call(
        _merge_kernel,
        out_shape=jax.ShapeDtypeStruct((n, d), BF16),
        grid=(n // tm,),
        in_specs=[pl.BlockSpec((tm, width), lambda i: (i, 0)),
                  pl.BlockSpec((tm, width), lambda i: (i, 0)),
                  pl.BlockSpec((tm, d), lambda i: (i, 0)),
                  pl.BlockSpec((tm, d), lambda i: (i, 1)),
                  const((width, d)), const((width, d))],
        out_specs=pl.BlockSpec((tm, d), lambda i: (i, 0)),
        compiler_params=_params("arbitrary"),
        name="merge_branches",
    )(oa, ob, gates, gates, wa, wb)


def _post_kernel(m_ref, x_ref, gm_ref, scf_ref, shf_ref, gffn_ref,
                 wo_ref, wrh_ref, wrl_ref, br_ref,
                 x1_ref, h2_ref, lg_ref):
    y = jnp.dot(m_ref[...], wo_ref[...], preferred_element_type=F32)
    x1 = x_ref[...] + gm_ref[...] * y
    x1_ref[...] = x1
    ms = jnp.mean(x1 * x1, axis=-1, keepdims=True)
    h2 = (x1 * lax.rsqrt(ms + EPS) * gffn_ref[...]) * (1.0 + scf_ref[...]) + shf_ref[...]
    h2_ref[...] = h2
    hb = h2.astype(BF16)
    hl = (h2 - hb.astype(F32)).astype(BF16)
    wrh = wrh_ref[...]
    lg = (lax.dot_general(wrh, hb, NT_DIMS, preferred_element_type=F32)
          + lax.dot_general(wrh, hl, NT_DIMS, preferred_element_type=F32)
          + lax.dot_general(wrl_ref[...], hb, NT_DIMS, preferred_element_type=F32))
    lg_ref[...] = lg + br_ref[...]


def _post_attention(merged, x2d, mod3, g_ffn, wo, wrh, wrl, br, seq):
    n, d = x2d.shape
    tm = 512
    per_b = seq // tm
    const = lambda shape: pl.BlockSpec(shape, lambda i: (0,) * len(shape), pipeline_mode=pl.Buffered(1))
    modspec = lambda idx: pl.BlockSpec((None, 1, d), lambda i: (i // per_b, 0, idx))
    return pl.pallas_call(
        _post_kernel,
        out_shape=(jax.ShapeDtypeStruct((n, d), F32),
                   jax.ShapeDtypeStruct((n, d), F32),
                   jax.ShapeDtypeStruct((LANES, n), F32)),
        grid=(n // tm,),
        in_specs=[pl.BlockSpec((tm, d), lambda i: (i, 0)),
                  pl.BlockSpec((tm, d), lambda i: (i, 0)),
                  modspec(2), modspec(4), modspec(3),
                  const((1, d)),
                  const((d, d)),
                  const((LANES, d)), const((LANES, d)), const((LANES, 1))],
        out_specs=(pl.BlockSpec((tm, d), lambda i: (i, 0)),
                   pl.BlockSpec((tm, d), lambda i: (i, 0)),
                   pl.BlockSpec((LANES, tm), lambda i: (0, i))),
        compiler_params=_params("arbitrary"),
        name="post_attention",
    )(merged, x2d, mod3, mod3, mod3, g_ffn.reshape(1, d), wo, wrh, wrl, br)


def _first_index_of_max(vals, iota, nrows):
    mx = jnp.max(vals, axis=0, keepdims=True)
    idx = jnp.min(jnp.where(vals == mx, iota, nrows), axis=0, keepdims=True)
    return mx, idx


def _route_kernel(lg_ref, e_ref, w_ref):
    g = N_GROUPS
    epg = EXPERTS_PER_GROUP
    lg = lg_ref[...]
    gl = lg[0:g, :]
    iota = lax.broadcasted_iota(I32, gl.shape, 0)
    gmax, gidx = _first_index_of_max(gl, iota, g)
    p_g = 1.0 / jnp.sum(jnp.exp(gl - gmax), axis=0, keepdims=True)

    esel = jnp.zeros((epg, lg.shape[1]), F32)
    for gi in range(g):
        esel = jnp.where(gidx == gi, lg[g + gi * epg:g + (gi + 1) * epg, :], esel)
    emax = jnp.max(esel, axis=0, keepdims=True)
    ex = jnp.exp(esel - emax)
    prob = ex / jnp.sum(ex, axis=0, keepdims=True)

    p0, i0 = _first_index_of_max(prob, iota, epg)
    rest = jnp.where(iota == i0, -1.0, prob)
    p1, i1 = _first_index_of_max(rest, iota, epg)
    tot = p0 + p1
    e_ref[0:1, :] = gidx * epg + i0
    e_ref[1:2, :] = gidx * epg + i1
    w_ref[0:1, :] = p_g * (p0 / tot)
    w_ref[1:2, :] = p_g * (p1 / tot)


def _route(logits_t):
    rows, n = logits_t.shape
    tn = 1024
    return pl.pallas_call(
        _route_kernel,
        out_shape=(jax.ShapeDtypeStruct((2, n), I32), jax.ShapeDtypeStruct((2, n), F32)),
        grid=(n // tn,),
        in_specs=[pl.BlockSpec((rows, tn), lambda i: (0, i))],
        out_specs=(pl.BlockSpec((2, tn), lambda i: (0, i)), pl.BlockSpec((2, tn), lambda i: (0, i))),
        compiler_params=_params("arbitrary"),
        name="route",
    )(logits_t)


VISIT_ROWS = 512
MOE_TILE = 256
MOE_FCHUNK = 512
GATHER_CHUNK = 32


def _moe_kernel(ve_ref, vnt_ref, vcnt_ref,
                tokc_ref, tokn_ref, h_hbm, wg_ref, wu_ref, wd_ref,
                y_ref,
                xbuf, sem, wgb, wub, wdb, *, nf, nv):
    v = pl.program_id(0)
    f = pl.program_id(1)
    slot = v % 2
    nt = vnt_ref[v]
    chunks_per_step = VISIT_ROWS // nf // GATHER_CHUNK

    def n_chunks(vv):
        return lax.shift_right_logical(vcnt_ref[vv] + (GATHER_CHUNK - 1), GATHER_CHUNK.bit_length() - 1)

    def row_copy(tok_ref, dst_slot, r):
        tok = tok_ref[0, r]
        return pltpu.make_async_copy(h_hbm.at[pl.ds(tok, 1)], xbuf.at[dst_slot, pl.ds(r, 1)],
                                     sem.at[dst_slot])

    def issue_chunks(tok_ref, dst_slot, c0, n):
        def chunk(c, carry):
            base = (c0 + c) * GATHER_CHUNK

            def body(r, cc):
                row_copy(tok_ref, dst_slot, base + r).start()
                return cc
            lax.fori_loop(0, GATHER_CHUNK, body, 0, unroll=8)
            return carry
        lax.fori_loop(0, n, chunk, 0)

    def wait_chunks(dst_slot, n):
        def chunk(c, carry):
            pltpu.make_async_copy(h_hbm.at[pl.ds(0, GATHER_CHUNK)],
                                  xbuf.at[dst_slot, pl.ds(0, GATHER_CHUNK)], sem.at[dst_slot]).wait()
            return carry
        lax.fori_loop(0, n, chunk, 0)

    @pl.when(jnp.logical_and(v == 0, f == 0))
    def _():
        xbuf[...] = jnp.zeros_like(xbuf)
        issue_chunks(tokc_ref, 0, 0, n_chunks(0))

    @pl.when(jnp.logical_and(f == 0, nt > 0))
    def _():
        wait_chunks(slot, n_chunks(v))

    @pl.when(f == 0)
    def _():
        y_ref[...] = jnp.zeros_like(y_ref)

    nxt = jnp.minimum(v + 1, nv - 1)
    chunks_next = jnp.where(v + 1 < nv, n_chunks(nxt), 0)
    c0 = f * chunks_per_step
    issue_chunks(tokn_ref, 1 - slot, c0, jnp.clip(chunks_next - c0, 0, chunks_per_step))

    @pl.when(nt > 0)
    def _():
        wgb[...] = wg_ref[...].astype(BF16)
        wub[...] = wu_ref[...].astype(BF16)
        wdb[...] = wd_ref[...].astype(BF16)
        for tl in range(VISIT_ROWS // MOE_TILE):
            @pl.when(tl < nt)
            def _():
                rows = pl.ds(tl * MOE_TILE, MOE_TILE)
                x = xbuf[slot, rows, :].astype(BF16)
                a = jnp.dot(x, wgb[...], preferred_element_type=F32)
                u = jnp.dot(x, wub[...], preferred_element_type=F32)
                hmid = ((a * _sigmoid(a)) * u).astype(BF16)
                y_ref[rows, :] += jnp.dot(hmid, wdb[...], preferred_element_type=F32)


def _moe_experts(h2, row_tok, vis_e, vis_nt, vis_cnt, w_gate, w_up, w_down):
    n, d = h2.shape
    n_exp, _, dexp = w_gate.shape
    nv = vis_e.shape[0]
    nf = dexp // MOE_FCHUNK
    assert GATHER_CHUNK & (GATHER_CHUNK - 1) == 0 and VISIT_ROWS % (nf * GATHER_CHUNK) == 0
    tok3 = row_tok.reshape(nv, 1, VISIT_ROWS)

    def fidx(v, f, vnt):
        return jnp.where(vnt[v] > 0, f, nf - 1)

    grid_spec = pltpu.PrefetchScalarGridSpec(
        num_scalar_prefetch=3,
        grid=(nv, nf),
        in_specs=[
            pl.BlockSpec((None, 1, VISIT_ROWS), lambda v, f, ve, vnt, vcnt: (v, 0, 0),
                         memory_space=pltpu.SMEM),
            pl.BlockSpec((None, 1, VISIT_ROWS), lambda v, f, ve, vnt, vcnt: (jnp.minimum(v + 1, nv - 1), 0, 0),
                         memory_space=pltpu.SMEM),
            pl.BlockSpec(memory_space=pl.ANY),
            pl.BlockSpec((None, d, MOE_FCHUNK), lambda v, f, ve, vnt, vcnt: (ve[v], 0, fidx(v, f, vnt))),
            pl.BlockSpec((None, d, MOE_FCHUNK), lambda v, f, ve, vnt, vcnt: (ve[v], 0, fidx(v, f, vnt))),
            pl.BlockSpec((None, MOE_FCHUNK, d), lambda v, f, ve, vnt, vcnt: (ve[v], fidx(v, f, vnt), 0)),
        ],
        out_specs=pl.BlockSpec((VISIT_ROWS, d), lambda v, f, ve, vnt, vcnt: (v, 0)),
        scratch_shapes=[pltpu.VMEM((2, VISIT_ROWS, d), F32),
                        pltpu.SemaphoreType.DMA((2,)),
                        pltpu.VMEM((d, MOE_FCHUNK), BF16),
                        pltpu.VMEM((d, MOE_FCHUNK), BF16),
                        pltpu.VMEM((MOE_FCHUNK, d), BF16)],
    )
    return pl.pallas_call(
        functools.partial(_moe_kernel, nf=nf, nv=nv),
        out_shape=jax.ShapeDtypeStruct((nv * VISIT_ROWS, d), F32),
        grid_spec=grid_spec,
        compiler_params=_params("arbitrary", "arbitrary"),
        name="moe_experts",
    )(vis_e, vis_nt, vis_cnt, tok3, tok3, h2, w_gate, w_up, w_down)


RANK_BLOCK = 256


def _slot_kernel(e_ref, dest_ref, cnt_ref, rank_ref, *, n_exp):
    nk, n = e_ref.shape
    eids = lax.broadcasted_iota(I32, (n_exp, RANK_BLOCK), 0)
    row = lax.broadcasted_iota(I32, (RANK_BLOCK, RANK_BLOCK), 0)
    col = lax.broadcasted_iota(I32, (RANK_BLOCK, RANK_BLOCK), 1)
    before = (row < col).astype(BF16)
    carry = jnp.zeros((n_exp, 1), F32)
    for k in range(nk):
        for c0 in range(0, n, RANK_BLOCK):
            hit = eids == e_ref[k:k + 1, c0:c0 + RANK_BLOCK]
            onehot = jnp.where(hit, 1.0, 0.0)
            earlier = jnp.dot(onehot.astype(BF16), before, preferred_element_type=F32) + carry
            rank = jnp.sum(jnp.where(hit, earlier, 0.0), axis=0, keepdims=True)
            rank_ref[k:k + 1, c0:c0 + RANK_BLOCK] = rank.astype(I32)
            carry = carry + jnp.sum(onehot, axis=1, keepdims=True)
    cnt_ref[...] = jnp.broadcast_to(carry, cnt_ref.shape).astype(I32)

    n_vis = lax.shift_right_logical(carry.astype(I32) + (VISIT_ROWS - 1),
                                    VISIT_ROWS.bit_length() - 1).astype(F32)
    ei = lax.broadcasted_iota(I32, (n_exp, n_exp), 0)
    ej = lax.broadcasted_iota(I32, (n_exp, n_exp), 1)
    lower = (ej < ei).astype(BF16)
    first_row = jnp.dot(lower, jnp.broadcast_to(n_vis, (n_exp, LANES)).astype(BF16),
                        preferred_element_type=F32)[:, 0:1] * float(VISIT_ROWS)
    for k in range(nk):
        for c0 in range(0, n, RANK_BLOCK):
            hit = eids == e_ref[k:k + 1, c0:c0 + RANK_BLOCK]
            base = jnp.sum(jnp.where(hit, first_row, 0.0), axis=0, keepdims=True)
            dest_ref[k:k + 1, c0:c0 + RANK_BLOCK] = rank_ref[k:k + 1, c0:c0 + RANK_BLOCK] + base.astype(I32)


def _assignment_slots(e2, n_exp):
    nk, n = e2.shape
    assert (n // VISIT_ROWS * nk + n_exp) <= 256, "visit counts must stay exact in bf16"
    return pl.pallas_call(
        functools.partial(_slot_kernel, n_exp=n_exp),
        out_shape=(jax.ShapeDtypeStruct((nk, n), I32), jax.ShapeDtypeStruct((n_exp, LANES), I32)),
        scratch_shapes=[pltpu.VMEM((nk, n), I32)],
        compiler_params=pltpu.CompilerParams(vmem_limit_bytes=VMEM_LIMIT_BYTES),
        name="assignment_slots",
    )(e2)


def _invert_kernel(dest_ref, tok_ref, zeros_vmem, sem):
    nk, n = dest_ref.shape
    zeros_vmem[...] = jnp.zeros_like(zeros_vmem)
    fill = pltpu.make_async_copy(zeros_vmem, tok_ref, sem)
    fill.start()
    fill.wait()

    def body(t, c):
        for k in range(nk):
            tok_ref[dest_ref[k, t]] = t
        return c
    lax.fori_loop(0, n, body, 0, unroll=8)


def _slot_tokens(dest2, n_slots):
    return pl.pallas_call(
        _invert_kernel,
        out_shape=jax.ShapeDtypeStruct((n_slots,), I32),
        in_specs=[pl.BlockSpec(memory_space=pltpu.SMEM)],
        out_specs=pl.BlockSpec(memory_space=pltpu.SMEM),
        scratch_shapes=[pltpu.VMEM((n_slots,), I32), pltpu.SemaphoreType.DMA],
        name="slot_tokens",
    )(dest2)


def _expert_layout(e2, n_exp):
    n = e2.shape[1]
    m = 2 * n
    nv = m // VISIT_ROWS + n_exp
    dest2, cnt = _assignment_slots(e2, n_exp)
    counts = cnt[:, 0]
    n_vis = (counts + VISIT_ROWS - 1) // VISIT_ROWS
    cum_vis = jnp.cumsum(n_vis)
    vbase = cum_vis - n_vis
    row_tok = _slot_tokens(dest2, nv * VISIT_ROWS)

    n_used = cum_vis[-1]
    vid = jnp.arange(nv, dtype=I32)
    used = vid < n_used
    ve = jnp.minimum(jnp.sum((cum_vis[None, :] <= vid[:, None]).astype(I32), axis=1), n_exp - 1)
    rem = counts[ve] - (vid - vbase[ve]) * VISIT_ROWS
    nt = jnp.clip((rem + MOE_TILE - 1) // MOE_TILE, 0, VISIT_ROWS // MOE_TILE)
    last = jnp.maximum(n_used - 1, 0)
    vis_nt = jnp.where(used, nt, 0).astype(I32)
    vis_cnt = jnp.where(used, jnp.clip(rem, 0, VISIT_ROWS), 0).astype(I32)
    vis_e = jnp.where(used, ve, ve[last]).astype(I32)
    return dest2, row_tok, vis_e, vis_nt, vis_cnt


def _final_kernel(posc_ref, posn_ref, x1_ref, gf_ref, w_ref, g_ref, ys_hbm, o_ref, ybuf, sem,
                  *, tm, nsteps):
    i = pl.program_id(0)
    slot = i % 2

    def row_copy(pos_ref, dst_slot, j):
        return pltpu.make_async_copy(ys_hbm.at[pl.ds(pos_ref[0, j], 1)],
                                     ybuf.at[dst_slot, pl.ds(j, 1)], sem.at[dst_slot])

    def issue(pos_ref, dst_slot):
        def body(j, c):
            row_copy(pos_ref, dst_slot, j).start()
            return c
        lax.fori_loop(0, 2 * tm, body, 0, unroll=8)

    @pl.when(i == 0)
    def _():
        issue(posc_ref, 0)

    @pl.when(i + 1 < nsteps)
    def _():
        issue(posn_ref, 1 - slot)

    pltpu.make_async_copy(ys_hbm.at[pl.ds(0, 2 * tm)], ybuf.at[slot], sem.at[slot]).wait()

    w = w_ref[...]
    moe = w[:, 0:1] * ybuf[slot, 0:tm, :] + w[:, 1:2] * ybuf[slot, tm:2 * tm, :]
    x = x1_ref[...] + gf_ref[...] * moe
    ms = jnp.mean(x * x, axis=-1, keepdims=True)
    o_ref[...] = x * lax.rsqrt(ms + EPS) * g_ref[...]


def _combine_final(x1, ys, pos, gate_w, mod3, g_final, seq):
    n, d = x1.shape
    tm = 256
    nsteps = n // tm
    per_b = seq // tm
    pos3 = pos.reshape(2, nsteps, tm).transpose(1, 0, 2).reshape(nsteps, 1, 2 * tm)
    return pl.pallas_call(
        functools.partial(_final_kernel, tm=tm, nsteps=nsteps),
        out_shape=jax.ShapeDtypeStruct((n, d), F32),
        grid=(nsteps,),
        in_specs=[pl.BlockSpec((None, 1, 2 * tm), lambda i: (i, 0, 0), memory_space=pltpu.SMEM),
                  pl.BlockSpec((None, 1, 2 * tm), lambda i: (jnp.minimum(i + 1, nsteps - 1), 0, 0),
                               memory_space=pltpu.SMEM),
                  pl.BlockSpec((tm, d), lambda i: (i, 0)),
                  pl.BlockSpec((None, 1, d), lambda i: (i // per_b, 0, 5)),
                  pl.BlockSpec((tm, 2), lambda i: (i, 0)),
                  pl.BlockSpec((1, d), lambda i: (0, 0)),
                  pl.BlockSpec(memory_space=pl.ANY)],
        out_specs=pl.BlockSpec((tm, d), lambda i: (i, 0)),
        scratch_shapes=[pltpu.VMEM((2, 2 * tm, d), F32), pltpu.SemaphoreType.DMA((2,))],
        compiler_params=_params("arbitrary"),
        name="combine_final",
    )(pos3, pos3, x1, mod3, gate_w, g_final.reshape(1, d), ys)


def kernel(x, c, rel_bias_table, w_ada, b_ada, g_mix, w_in, lambda_q1, lambda_k1, lambda_q2,
           lambda_k2, g_subln, w_proj_a, w_proj_b, w_out, g_ffn, w_router_group, b_router_group,
           w_router_expert, b_router_expert, w_expert_gate, w_expert_up, w_expert_down, g_final):
    bsz, seq, d = x.shape
    n = bsz * seq
    depth = w_in.shape[0]
    assert depth == 1, "the MoE combine is fused with the final RMSNorm: one layer only"
    da_width = DA_HEADS * HEAD_DIM
    sb_width = SB_HEADS * HEAD_DIM
    qkv_cols = 3 * da_width + 3 * sb_width
    attn_tile = 256
    attn_qpt = seq // attn_tile
    n_exp = w_expert_gate.shape[1]
    xf = x.reshape(n, d)

    for l in range(depth):
        lam_init = 0.8 - 0.6 * math.exp(-0.3 * l)
        mod = _adaln_mod(c, w_ada[l:l + 1], b_ada[l])
        mod3 = mod.reshape(bsz, 1, N_MOD * d)

        h = _norm_modulate(xf.reshape(bsz, seq, d), g_mix[l], mod3, 1, 0).reshape(n, d)
        qkv = _in_proj(h, w_in[l:l + 1], 0, qkv_cols, BF16, gate=False)
        gates = _in_proj(h, w_in[l:l + 1], qkv_cols, 2 * d, F32, gate=True)
        qkv3 = qkv.reshape(bsz, seq, qkv_cols)

        lam_params = jnp.stack([lambda_q1[l], lambda_k1[l], lambda_q2[l], lambda_k2[l]]).astype(F32)
        bias = _bias_rows(rel_bias_table, attn_tile)
        oa = _diff_attention(qkv3, lam_params, g_subln[l], bias, lam_init, attn_tile, attn_qpt)
        ob = _sb_attention(qkv3, 3 * DA_HEADS, attn_tile, attn_qpt)

        w_r = jnp.concatenate([w_router_group[l], w_router_expert[l]], axis=1).astype(F32).T
        w_r = jnp.pad(w_r, ((0, LANES - w_r.shape[0]), (0, 0)))
        wrh = w_r.astype(BF16)
        wrl = (w_r - wrh.astype(F32)).astype(BF16)
        b_r = jnp.concatenate([b_router_group[l], b_router_expert[l]]).astype(F32)
        b_r = jnp.pad(b_r, (0, LANES - b_r.shape[0])).reshape(LANES, 1)
        merged = _merge_branches(oa.reshape(n, da_width), ob.reshape(n, sb_width), gates,
                                 w_proj_a[l].astype(BF16), w_proj_b[l].astype(BF16))
        x1, h2, logits_t = _post_attention(merged, xf, mod3, g_ffn[l], w_out[l].astype(BF16),
                                           wrh, wrl, b_r, seq)

        e2, g2 = _route(logits_t)
        pos, row_tok, vis_e, vis_nt, vis_cnt = _expert_layout(e2, n_exp)
        ys = _moe_experts(h2, row_tok, vis_e, vis_nt, vis_cnt,
                          w_expert_gate[l], w_expert_up[l], w_expert_down[l])
        out = _combine_final(x1, ys, pos, g2.T, mod3, g_final, seq)
    return out.reshape(bsz, seq, d)
```

```python
import functools
import math

import jax
import jax.numpy as jnp
from jax import lax
from jax.experimental import pallas as pl
from jax.experimental.pallas import tpu as pltpu

F32 = jnp.float32
BF16 = jnp.bfloat16
I32 = jnp.int32
EPS = 1e-6

DA_HEADS = 8
DA_HALF_DIM = 64
SB_HEADS = 8
HEAD_DIM = 128
REL_BUCKETS = 32
REL_MAX_DIST = 128
N_GROUPS = 8
EXPERTS_PER_GROUP = 8
N_MOD = 6

VMEM_LIMIT_BYTES = 56 * 1024 * 1024
LANES = 128

ADALN_COLS = 1024
NORM_ROWS = 512
PROJ_ROWS = 1024
PROJ_COLS = 1024
ATTN_TILE = 256
MERGE_ROWS = 512
POST_ROWS = 512
ROUTE_COLS = 1024
FINAL_ROWS = 256

NT_DIMS = (((1,), (1,)), ((), ()))


def _params(*sem):
    return pltpu.CompilerParams(dimension_semantics=sem, vmem_limit_bytes=VMEM_LIMIT_BYTES)


def _sigmoid(v):
    return 1.0 / (1.0 + jnp.exp(-v))


def _mod_kernel(c_ref, w_ref, b_ref, o_ref):
    c = c_ref[...]
    s = (c * _sigmoid(c)).astype(BF16)
    o_ref[...] = jnp.dot(s, w_ref[...].astype(BF16), preferred_element_type=F32) + b_ref[...]


def _adaln_mod(c, w_ada, b_ada):
    bsz, d = c.shape
    ncol = w_ada.shape[-1]
    tn = ADALN_COLS
    return pl.pallas_call(
        _mod_kernel,
        out_shape=jax.ShapeDtypeStruct((bsz, ncol), F32),
        grid=(ncol // tn,),
        in_specs=[pl.BlockSpec((bsz, d), lambda j: (0, 0)),
                  pl.BlockSpec((None, d, tn), lambda j: (0, 0, j)),
                  pl.BlockSpec((1, tn), lambda j: (0, j))],
        out_specs=pl.BlockSpec((bsz, tn), lambda j: (0, j)),
        compiler_params=_params("arbitrary"),
        name="adaln_mod",
    )(c, w_ada, b_ada.reshape(1, ncol))


def _hnorm_kernel(x_ref, g_ref, sc_ref, sh_ref, o_ref):
    x = x_ref[...]
    ms = jnp.mean(x * x, axis=-1, keepdims=True)
    y = x * lax.rsqrt(ms + EPS) * g_ref[...]
    o_ref[...] = (y * (1.0 + sc_ref[...]) + sh_ref[...]).astype(o_ref.dtype)


def _norm_modulate(x, g, mod3, scale_idx, shift_idx):
    bsz, s, d = x.shape
    ts = NORM_ROWS
    return pl.pallas_call(
        _hnorm_kernel,
        out_shape=jax.ShapeDtypeStruct((bsz, s, d), BF16),
        grid=(bsz, s // ts),
        in_specs=[pl.BlockSpec((None, ts, d), lambda b, i: (b, i, 0)),
                  pl.BlockSpec((1, d), lambda b, i: (0, 0)),
                  pl.BlockSpec((None, 1, d), lambda b, i: (b, 0, scale_idx)),
                  pl.BlockSpec((None, 1, d), lambda b, i: (b, 0, shift_idx))],
        out_specs=pl.BlockSpec((None, ts, d), lambda b, i: (b, i, 0)),
        compiler_params=_params("arbitrary", "arbitrary"),
        name="norm_modulate",
    )(x, g.reshape(1, d), mod3, mod3)


def _proj_kernel(h_ref, w_ref, o_ref, wb_ref, *, gate):
    @pl.when(pl.program_id(1) == 0)
    def _():
        wb_ref[...] = w_ref[...].astype(BF16)

    r = jnp.dot(h_ref[...], wb_ref[...], preferred_element_type=F32)
    if gate:
        r = _sigmoid(r)
    o_ref[...] = r.astype(o_ref.dtype)


def _in_proj(h2d, w_in, col0, ncols, out_dtype, gate):
    n, d = h2d.shape
    tn, tm = PROJ_COLS, PROJ_ROWS
    jb = col0 // tn
    return pl.pallas_call(
        functools.partial(_proj_kernel, gate=gate),
        out_shape=jax.ShapeDtypeStruct((n, ncols), out_dtype),
        grid=(ncols // tn, n // tm),
        in_specs=[pl.BlockSpec((tm, d), lambda j, i: (i, 0)),
                  pl.BlockSpec((None, d, tn), lambda j, i: (0, 0, j + jb))],
        out_specs=pl.BlockSpec((tm, tn), lambda j, i: (i, j)),
        scratch_shapes=[pltpu.VMEM((d, tn), BF16)],
        compiler_params=_params("arbitrary", "arbitrary"),
        name="in_proj_gate" if gate else "in_proj_qkv",
    )(h2d, w_in)


def _rel_bucket(n):
    n = jnp.maximum(n, 0)
    max_exact = REL_BUCKETS // 2
    nf = jnp.maximum(n, 1).astype(F32)
    large = max_exact + (jnp.log(nf / max_exact) / math.log(REL_MAX_DIST / max_exact)
                         * (REL_BUCKETS - max_exact)).astype(I32)
    large = jnp.minimum(large, REL_BUCKETS - 1)
    return jnp.where(n < max_exact, n, large)


def _bias_rows(rel_table, t):
    assert 2 * t - (t - 1) >= REL_MAX_DIST, "far tiles must sit wholly in the last bucket"
    nb, h = rel_table.shape
    dist = jnp.arange(-(t - 1), 3 * t, dtype=I32)
    by_dist = rel_table[_rel_bucket(dist)].astype(F32).T
    rows = [jnp.pad(by_dist[:, delta * t:delta * t + 2 * t - 1][:, ::-1], ((0, 0), (0, 1)))
            for delta in range(2)]
    rows.append(jnp.broadcast_to(by_dist[:, -1:], (h, 2 * t)))
    return jnp.stack(rows, axis=1)


def _lane_chunks(x):
    return [x[:, c * LANES:(c + 1) * LANES] for c in range(x.shape[1] // LANES)]


def _diff_attn_kernel(lam_ref, g_ref, brow_ref, q_ref, k_ref, v_ref, o_ref, s1_ref, s2_ref, bias_ref,
                      *, t, nsteps, qpt, lam_init):
    step = pl.program_id(2)
    lp = lam_ref[...]
    lam = (jnp.exp(jnp.sum(lp[0:1] * lp[1:2], axis=-1, keepdims=True))
           - jnp.exp(jnp.sum(lp[2:3] * lp[3:4], axis=-1, keepdims=True)) + lam_init)
    lane = lax.broadcasted_iota(I32, (t, HEAD_DIM), 1)

    qrow = lax.broadcasted_iota(I32, (t, t), 0)
    kcol = lax.broadcasted_iota(I32, (t, t), 1)
    for delta in range(2):
        gen = jnp.broadcast_to(brow_ref[delta:delta + 1, :], (t, 2 * t))
        tile = pltpu.roll(gen, t + 1, 1, stride=1, stride_axis=0)[:, :t]
        if delta == 0:
            tile = jnp.where(kcol <= qrow, tile, -jnp.inf)
        bias_ref[delta] = tile
    bias_ref[2] = jnp.broadcast_to(brow_ref[2:3, :t], (t, t))

    def scores(qh, s_ref, nk):
        mx = None
        for j in range(nk):
            s = (lax.dot_general(qh, k_ref[j * t:(j + 1) * t, :], NT_DIMS, preferred_element_type=F32)
                 + bias_ref[min(nk - 1 - j, 2)])
            s_ref[:, j * t:(j + 1) * t] = s
            for ch in _lane_chunks(s):
                mx = ch if mx is None else jnp.maximum(mx, ch)
        return jnp.broadcast_to(jnp.max(mx, axis=-1, keepdims=True), (t, LANES))

    def softmax_pv(s_ref, mb, nk):
        lsum = jnp.zeros((t, LANES), F32)
        acc = jnp.zeros((t, HEAD_DIM), F32)
        for j in range(nk):
            ps = [jnp.exp(ch - mb) for ch in _lane_chunks(s_ref[:, j * t:(j + 1) * t])]
            for p in ps:
                lsum = lsum + p
            acc = acc + jnp.dot(jnp.concatenate(ps, axis=1).astype(BF16), v_ref[j * t:(j + 1) * t, :],
                                preferred_element_type=F32)
        return acc / jnp.sum(lsum, axis=-1, keepdims=True)

    def q_tile(lt, nk):
        rows = slice(lt * t, (lt + 1) * t)
        q = q_ref[rows, :] * jnp.asarray(DA_HALF_DIM ** -0.5, BF16)
        q1 = jnp.where(lane < DA_HALF_DIM, q, jnp.zeros_like(q))
        q2 = jnp.where(lane >= DA_HALF_DIM, q, jnp.zeros_like(q))
        s1, s2 = s1_ref.at[lt % 2], s2_ref.at[lt % 2]
        m1 = scores(q1, s1, nk)
        m2 = scores(q2, s2, nk)
        o = softmax_pv(s1, m1, nk) - lam * softmax_pv(s2, m2, nk)
        ms = jnp.mean(o * o, axis=-1, keepdims=True)
        o = (o * lax.rsqrt(ms + EPS) * g_ref[...]) * (1.0 - lam_init)
        o_ref[rows, :] = o.astype(o_ref.dtype)

    def variant(first_tile):
        for lt in range(qpt):
            q_tile(lt, first_tile + lt + 1)

    for i in range(nsteps):
        pl.when(step == i)(functools.partial(variant, i * qpt))


def _diff_attention(qkv3, lam_params, g_subln, bias, lam_init, t, qpt):
    bsz, s, _ = qkv3.shape
    h = DA_HEADS
    nsteps = s // (t * qpt)
    return pl.pallas_call(
        functools.partial(_diff_attn_kernel, t=t, nsteps=nsteps, qpt=qpt, lam_init=lam_init),
        out_shape=jax.ShapeDtypeStruct((bsz, s, h * HEAD_DIM), BF16),
        scratch_shapes=[pltpu.VMEM((2, t, s), F32), pltpu.VMEM((2, t, s), F32), pltpu.VMEM((3, t, t), F32)],
        grid=(bsz, h, nsteps),
        in_specs=[pl.BlockSpec((4, DA_HALF_DIM), lambda b, hh, i: (0, 0)),
                  pl.BlockSpec((1, HEAD_DIM), lambda b, hh, i: (0, 0)),
                  pl.BlockSpec((None, 3, 2 * t), lambda b, hh, i: (hh, 0, 0)),
                  pl.BlockSpec((None, t * qpt, HEAD_DIM), lambda b, hh, i: (b, i, hh)),
                  pl.BlockSpec((None, s, HEAD_DIM), lambda b, hh, i: (b, 0, h + hh)),
                  pl.BlockSpec((None, s, HEAD_DIM), lambda b, hh, i: (b, 0, 2 * h + hh))],
        out_specs=pl.BlockSpec((None, t * qpt, HEAD_DIM), lambda b, hh, i: (b, i, hh)),
        compiler_params=_params("arbitrary", "arbitrary", "arbitrary"),
        name="diff_attention",
    )(lam_params, g_subln.reshape(1, HEAD_DIM), bias, qkv3, qkv3, qkv3)


def _sb_attn_kernel(q_ref, k_ref, v_ref, o_ref, e_ref, *, t, nsteps, qpt, scale):
    step = pl.program_id(2)
    row = lax.broadcasted_iota(I32, (t, t), 0)
    col = lax.broadcasted_iota(I32, (t, t), 1)
    neg_tri = jnp.where(row > col, -1.0, 0.0).astype(BF16)
    past = col < row

    def q_tile(lt, nk):
        rows = slice(lt * t, (lt + 1) * t)
        q = q_ref[rows, :]
        e = e_ref.at[lt % 2]
        totals = []
        for j in range(nk):
            z = lax.dot_general(q, k_ref[j * t:(j + 1) * t, :], NT_DIMS, preferred_element_type=F32) * scale
            softplus = jnp.maximum(z, 0.0) + jnp.log(1.0 + jnp.exp2(jnp.abs(z) * (-math.log2(math.e))))
            log_beta = z - softplus
            if j == nk - 1:
                softplus = jnp.where(past, softplus, 0.0)
            excl = jnp.dot(softplus.astype(BF16), neg_tri, preferred_element_type=F32)
            e[:, j * t:(j + 1) * t] = log_beta + excl
            totals.append(excl[:, 0:1] - softplus[:, 0:1])

        acc = jnp.zeros((t, HEAD_DIM), F32)
        later = jnp.zeros((t, 1), F32)
        for j in reversed(range(nk)):
            lb = jnp.broadcast_to(later, (t, LANES))
            ws = [jnp.exp(ch + lb) for ch in _lane_chunks(e[:, j * t:(j + 1) * t])]
            w = jnp.concatenate(ws, axis=1)
            if j == nk - 1:
                w = jnp.where(past, w, 0.0)
            acc = acc + jnp.dot(w.astype(BF16), v_ref[j * t:(j + 1) * t, :], preferred_element_type=F32)
            later = later + totals[j]
        o_ref[rows, :] = acc.astype(o_ref.dtype)

    def variant(first_tile):
        for lt in range(qpt):
            q_tile(lt, first_tile + lt + 1)

    for i in range(nsteps):
        pl.when(step == i)(functools.partial(variant, i * qpt))


def _sb_attention(qkv3, col_block0, t, qpt):
    bsz, s, _ = qkv3.shape
    h = SB_HEADS
    nsteps = s // (t * qpt)
    return pl.pallas_call(
        functools.partial(_sb_attn_kernel, t=t, nsteps=nsteps, qpt=qpt, scale=HEAD_DIM ** -0.5),
        out_shape=jax.ShapeDtypeStruct((bsz, s, h * HEAD_DIM), BF16),
        scratch_shapes=[pltpu.VMEM((2, t, s), F32)],
        grid=(bsz, h, nsteps),
        in_specs=[pl.BlockSpec((None, t * qpt, HEAD_DIM), lambda b, hh, i: (b, i, col_block0 + hh)),
                  pl.BlockSpec((None, s, HEAD_DIM), lambda b, hh, i: (b, 0, col_block0 + h + hh)),
                  pl.BlockSpec((None, s, HEAD_DIM), lambda b, hh, i: (b, 0, col_block0 + 2 * h + hh))],
        out_specs=pl.BlockSpec((None, t * qpt, HEAD_DIM), lambda b, hh, i: (b, i, hh)),
        compiler_params=_params("arbitrary", "arbitrary", "arbitrary"),
        name="sb_attention",
    )(qkv3, qkv3, qkv3)


def _merge_kernel(oa_ref, ob_ref, sa_ref, sb_ref, wa_ref, wb_ref, m_ref):
    merged = (sa_ref[...] * jnp.dot(oa_ref[...], wa_ref[...], preferred_element_type=F32)
              + sb_ref[...] * jnp.dot(ob_ref[...], wb_ref[...], preferred_element_type=F32))
    m_ref[...] = merged.astype(m_ref.dtype)


def _merge_branches(oa, ob, gates, wa, wb):
    n, width = oa.shape
    d = wa.shape[1]
    tm = MERGE_ROWS
    const = lambda shape: pl.BlockSpec(shape, lambda i: (0,) * len(shape), pipeline_mode=pl.Buffered(1))
    return pl.pallas_call(
        _merge_kernel,
        out_shape=jax.ShapeDtypeStruct((n, d), BF16),
        grid=(n // tm,),
        in_specs=[pl.BlockSpec((tm, width), lambda i: (i, 0)),
                  pl.BlockSpec((tm, width), lambda i: (i, 0)),
                  pl.BlockSpec((tm, d), lambda i: (i, 0)),
                  pl.BlockSpec((tm, d), lambda i: (i, 1)),
                  const((width, d)), const((width, d))],
        out_specs=pl.BlockSpec((tm, d), lambda i: (i, 0)),
        compiler_params=_params("arbitrary"),
        name="merge_branches",
    )(oa, ob, gates, gates, wa, wb)


def _post_kernel(m_ref, x_ref, gm_ref, scf_ref, shf_ref, gffn_ref,
                 wo_ref, wrh_ref, wrl_ref, br_ref,
                 x1_ref, h2_ref, lg_ref):
    y = jnp.dot(m_ref[...], wo_ref[...], preferred_element_type=F32)
    x1 = x_ref[...] + gm_ref[...] * y
    x1_ref[...] = x1
    ms = jnp.mean(x1 * x1, axis=-1, keepdims=True)
    h2 = (x1 * lax.rsqrt(ms + EPS) * gffn_ref[...]) * (1.0 + scf_ref[...]) + shf_ref[...]
    h2_ref[...] = h2
    hb = h2.astype(BF16)
    hl = (h2 - hb.astype(F32)).astype(BF16)
    wrh = wrh_ref[...]
    lg = (lax.dot_general(wrh, hb, NT_DIMS, preferred_element_type=F32)
          + lax.dot_general(wrh, hl, NT_DIMS, preferred_element_type=F32)
          + lax.dot_general(wrl_ref[...], hb, NT_DIMS, preferred_element_type=F32))
    lg_ref[...] = lg + br_ref[...]


def _post_attention(merged, x2d, mod3, g_ffn, wo, wrh, wrl, br, seq):
    n, d = x2d.shape
    tm = POST_ROWS
    per_b = seq // tm
    const = lambda shape: pl.BlockSpec(shape, lambda i: (0,) * len(shape), pipeline_mode=pl.Buffered(1))
    modspec = lambda idx: pl.BlockSpec((None, 1, d), lambda i: (i // per_b, 0, idx))
    return pl.pallas_call(
        _post_kernel,
        out_shape=(jax.ShapeDtypeStruct((n, d), F32),
                   jax.ShapeDtypeStruct((n, d), F32),
                   jax.ShapeDtypeStruct((LANES, n), F32)),
        grid=(n // tm,),
        in_specs=[pl.BlockSpec((tm, d), lambda i: (i, 0)),
                  pl.BlockSpec((tm, d), lambda i: (i, 0)),
                  modspec(2), modspec(4), modspec(3),
                  const((1, d)),
                  const((d, d)),
                  const((LANES, d)), const((LANES, d)), const((LANES, 1))],
        out_specs=(pl.BlockSpec((tm, d), lambda i: (i, 0)),
                   pl.BlockSpec((tm, d), lambda i: (i, 0)),
                   pl.BlockSpec((LANES, tm), lambda i: (0, i))),
        compiler_params=_params("arbitrary"),
        name="post_attention",
    )(merged, x2d, mod3, mod3, mod3, g_ffn.reshape(1, d), wo, wrh, wrl, br)


def _first_index_of_max(vals, iota, nrows):
    mx = jnp.max(vals, axis=0, keepdims=True)
    idx = jnp.min(jnp.where(vals == mx, iota, nrows), axis=0, keepdims=True)
    return mx, idx


def _route_kernel(lg_ref, e_ref, w_ref):
    g = N_GROUPS
    epg = EXPERTS_PER_GROUP
    lg = lg_ref[...]
    gl = lg[0:g, :]
    iota = lax.broadcasted_iota(I32, gl.shape, 0)
    gmax, gidx = _first_index_of_max(gl, iota, g)
    p_g = 1.0 / jnp.sum(jnp.exp(gl - gmax), axis=0, keepdims=True)

    esel = jnp.zeros((epg, lg.shape[1]), F32)
    for gi in range(g):
        esel = jnp.where(gidx == gi, lg[g + gi * epg:g + (gi + 1) * epg, :], esel)
    emax = jnp.max(esel, axis=0, keepdims=True)
    ex = jnp.exp(esel - emax)
    prob = ex / jnp.sum(ex, axis=0, keepdims=True)

    p0, i0 = _first_index_of_max(prob, iota, epg)
    rest = jnp.where(iota == i0, -1.0, prob)
    p1, i1 = _first_index_of_max(rest, iota, epg)
    tot = p0 + p1
    e_ref[0:1, :] = gidx * epg + i0
    e_ref[1:2, :] = gidx * epg + i1
    w_ref[0:1, :] = p_g * (p0 / tot)
    w_ref[1:2, :] = p_g * (p1 / tot)


def _route(logits_t):
    rows, n = logits_t.shape
    tn = ROUTE_COLS
    return pl.pallas_call(
        _route_kernel,
        out_shape=(jax.ShapeDtypeStruct((2, n), I32), jax.ShapeDtypeStruct((2, n), F32)),
        grid=(n // tn,),
        in_specs=[pl.BlockSpec((rows, tn), lambda i: (0, i))],
        out_specs=(pl.BlockSpec((2, tn), lambda i: (0, i)), pl.BlockSpec((2, tn), lambda i: (0, i))),
        compiler_params=_params("arbitrary"),
        name="route",
    )(logits_t)


VISIT_ROWS = 512
MOE_TILE = 256
MOE_FCHUNK = 512
GATHER_CHUNK = 32


def _moe_kernel(ve_ref, vnt_ref, vcnt_ref,
                tokc_ref, tokn_ref, h_hbm, wg_ref, wu_ref, wd_ref,
                y_ref,
                xbuf, sem, wgb, wub, wdb, *, nf, nv):
    v = pl.program_id(0)
    f = pl.program_id(1)
    slot = v % 2
    nt = vnt_ref[v]
    chunks_per_step = VISIT_ROWS // nf // GATHER_CHUNK

    def n_chunks(vv):
        return lax.shift_right_logical(vcnt_ref[vv] + (GATHER_CHUNK - 1), GATHER_CHUNK.bit_length() - 1)

    def row_copy(tok_ref, dst_slot, r):
        tok = tok_ref[0, r]
        return pltpu.make_async_copy(h_hbm.at[pl.ds(tok, 1)], xbuf.at[dst_slot, pl.ds(r, 1)],
                                     sem.at[dst_slot])

    def issue_chunks(tok_ref, dst_slot, c0, n):
        def chunk(c, carry):
            base = (c0 + c) * GATHER_CHUNK

            def body(r, cc):
                row_copy(tok_ref, dst_slot, base + r).start()
                return cc
            lax.fori_loop(0, GATHER_CHUNK, body, 0, unroll=8)
            return carry
        lax.fori_loop(0, n, chunk, 0)

    def wait_chunks(dst_slot, n):
        def chunk(c, carry):
            pltpu.make_async_copy(h_hbm.at[pl.ds(0, GATHER_CHUNK)],
                                  xbuf.at[dst_slot, pl.ds(0, GATHER_CHUNK)], sem.at[dst_slot]).wait()
            return carry
        lax.fori_loop(0, n, chunk, 0)

    @pl.when(jnp.logical_and(v == 0, f == 0))
    def _():
        xbuf[...] = jnp.zeros_like(xbuf)
        issue_chunks(tokc_ref, 0, 0, n_chunks(0))

    @pl.when(jnp.logical_and(f == 0, nt > 0))
    def _():
        wait_chunks(slot, n_chunks(v))

    @pl.when(f == 0)
    def _():
        y_ref[...] = jnp.zeros_like(y_ref)

    nxt = jnp.minimum(v + 1, nv - 1)
    chunks_next = jnp.where(v + 1 < nv, n_chunks(nxt), 0)
    c0 = f * chunks_per_step
    issue_chunks(tokn_ref, 1 - slot, c0, jnp.clip(chunks_next - c0, 0, chunks_per_step))

    @pl.when(nt > 0)
    def _():
        wgb[...] = wg_ref[...].astype(BF16)
        wub[...] = wu_ref[...].astype(BF16)
        wdb[...] = wd_ref[...].astype(BF16)
        for tl in range(VISIT_ROWS // MOE_TILE):
            @pl.when(tl < nt)
            def _():
                rows = pl.ds(tl * MOE_TILE, MOE_TILE)
                x = xbuf[slot, rows, :].astype(BF16)
                a = jnp.dot(x, wgb[...], preferred_element_type=F32)
                u = jnp.dot(x, wub[...], preferred_element_type=F32)
                hmid = ((a * _sigmoid(a)) * u).astype(BF16)
                y_ref[rows, :] += jnp.dot(hmid, wdb[...], preferred_element_type=F32)


def _moe_experts(h2, row_tok, vis_e, vis_nt, vis_cnt, w_gate, w_up, w_down):
    n, d = h2.shape
    n_exp, _, dexp = w_gate.shape
    nv = vis_e.shape[0]
    nf = dexp // MOE_FCHUNK
    assert GATHER_CHUNK & (GATHER_CHUNK - 1) == 0 and VISIT_ROWS % (nf * GATHER_CHUNK) == 0
    tok3 = row_tok.reshape(nv, 1, VISIT_ROWS)

    def fidx(v, f, vnt):
        return jnp.where(vnt[v] > 0, f, nf - 1)

    grid_spec = pltpu.PrefetchScalarGridSpec(
        num_scalar_prefetch=3,
        grid=(nv, nf),
        in_specs=[
            pl.BlockSpec((None, 1, VISIT_ROWS), lambda v, f, ve, vnt, vcnt: (v, 0, 0),
                         memory_space=pltpu.SMEM),
            pl.BlockSpec((None, 1, VISIT_ROWS), lambda v, f, ve, vnt, vcnt: (jnp.minimum(v + 1, nv - 1), 0, 0),
                         memory_space=pltpu.SMEM),
            pl.BlockSpec(memory_space=pl.ANY),
            pl.BlockSpec((None, d, MOE_FCHUNK), lambda v, f, ve, vnt, vcnt: (ve[v], 0, fidx(v, f, vnt))),
            pl.BlockSpec((None, d, MOE_FCHUNK), lambda v, f, ve, vnt, vcnt: (ve[v], 0, fidx(v, f, vnt))),
            pl.BlockSpec((None, MOE_FCHUNK, d), lambda v, f, ve, vnt, vcnt: (ve[v], fidx(v, f, vnt), 0)),
        ],
        out_specs=pl.BlockSpec((VISIT_ROWS, d), lambda v, f, ve, vnt, vcnt: (v, 0)),
        scratch_shapes=[pltpu.VMEM((2, VISIT_ROWS, d), F32),
                        pltpu.SemaphoreType.DMA((2,)),
                        pltpu.VMEM((d, MOE_FCHUNK), BF16),
                        pltpu.VMEM((d, MOE_FCHUNK), BF16),
                        pltpu.VMEM((MOE_FCHUNK, d), BF16)],
    )
    return pl.pallas_call(
        functools.partial(_moe_kernel, nf=nf, nv=nv),
        out_shape=jax.ShapeDtypeStruct((nv * VISIT_ROWS, d), F32),
        grid_spec=grid_spec,
        compiler_params=_params("arbitrary", "arbitrary"),
        name="moe_experts",
    )(vis_e, vis_nt, vis_cnt, tok3, tok3, h2, w_gate, w_up, w_down)


RANK_BLOCK = 256


def _slot_kernel(e_ref, dest_ref, cnt_ref, rank_ref, *, n_exp):
    nk, n = e_ref.shape
    eids = lax.broadcasted_iota(I32, (n_exp, RANK_BLOCK), 0)
    row = lax.broadcasted_iota(I32, (RANK_BLOCK, RANK_BLOCK), 0)
    col = lax.broadcasted_iota(I32, (RANK_BLOCK, RANK_BLOCK), 1)
    before = (row < col).astype(BF16)
    carry = jnp.zeros((n_exp, 1), F32)
    for k in range(nk):
        for c0 in range(0, n, RANK_BLOCK):
            hit = eids == e_ref[k:k + 1, c0:c0 + RANK_BLOCK]
            onehot = jnp.where(hit, 1.0, 0.0)
            earlier = jnp.dot(onehot.astype(BF16), before, preferred_element_type=F32) + carry
            rank = jnp.sum(jnp.where(hit, earlier, 0.0), axis=0, keepdims=True)
            rank_ref[k:k + 1, c0:c0 + RANK_BLOCK] = rank.astype(I32)
            carry = carry + jnp.sum(onehot, axis=1, keepdims=True)
    cnt_ref[...] = jnp.broadcast_to(carry, cnt_ref.shape).astype(I32)

    n_vis = lax.shift_right_logical(carry.astype(I32) + (VISIT_ROWS - 1),
                                    VISIT_ROWS.bit_length() - 1).astype(F32)
    ei = lax.broadcasted_iota(I32, (n_exp, n_exp), 0)
    ej = lax.broadcasted_iota(I32, (n_exp, n_exp), 1)
    lower = (ej < ei).astype(BF16)
    first_row = jnp.dot(lower, jnp.broadcast_to(n_vis, (n_exp, LANES)).astype(BF16),
                        preferred_element_type=F32)[:, 0:1] * float(VISIT_ROWS)
    for k in range(nk):
        for c0 in range(0, n, RANK_BLOCK):
            hit = eids == e_ref[k:k + 1, c0:c0 + RANK_BLOCK]
            base = jnp.sum(jnp.where(hit, first_row, 0.0), axis=0, keepdims=True)
            dest_ref[k:k + 1, c0:c0 + RANK_BLOCK] = rank_ref[k:k + 1, c0:c0 + RANK_BLOCK] + base.astype(I32)


def _assignment_slots(e2, n_exp):
    nk, n = e2.shape
    assert (n // VISIT_ROWS * nk + n_exp) <= 256, "visit counts must stay exact in bf16"
    return pl.pallas_call(
        functools.partial(_slot_kernel, n_exp=n_exp),
        out_shape=(jax.ShapeDtypeStruct((nk, n), I32), jax.ShapeDtypeStruct((n_exp, LANES), I32)),
        scratch_shapes=[pltpu.VMEM((nk, n), I32)],
        compiler_params=pltpu.CompilerParams(vmem_limit_bytes=VMEM_LIMIT_BYTES),
        name="assignment_slots",
    )(e2)


def _invert_kernel(dest_ref, tok_ref, zeros_vmem, sem):
    nk, n = dest_ref.shape
    zeros_vmem[...] = jnp.zeros_like(zeros_vmem)
    fill = pltpu.make_async_copy(zeros_vmem, tok_ref, sem)
    fill.start()
    fill.wait()

    def body(t, c):
        for k in range(nk):
            tok_ref[dest_ref[k, t]] = t
        return c
    lax.fori_loop(0, n, body, 0, unroll=8)


def _slot_tokens(dest2, n_slots):
    return pl.pallas_call(
        _invert_kernel,
        out_shape=jax.ShapeDtypeStruct((n_slots,), I32),
        in_specs=[pl.BlockSpec(memory_space=pltpu.SMEM)],
        out_specs=pl.BlockSpec(memory_space=pltpu.SMEM),
        scratch_shapes=[pltpu.VMEM((n_slots,), I32), pltpu.SemaphoreType.DMA],
        name="slot_tokens",
    )(dest2)


def _expert_layout(e2, n_exp):
    n = e2.shape[1]
    m = 2 * n
    nv = m // VISIT_ROWS + n_exp
    dest2, cnt = _assignment_slots(e2, n_exp)
    counts = cnt[:, 0]
    n_vis = (counts + VISIT_ROWS - 1) // VISIT_ROWS
    cum_vis = jnp.cumsum(n_vis)
    vbase = cum_vis - n_vis
    row_tok = _slot_tokens(dest2, nv * VISIT_ROWS)

    n_used = cum_vis[-1]
    vid = jnp.arange(nv, dtype=I32)
    used = vid < n_used
    ve = jnp.minimum(jnp.sum((cum_vis[None, :] <= vid[:, None]).astype(I32), axis=1), n_exp - 1)
    rem = counts[ve] - (vid - vbase[ve]) * VISIT_ROWS
    nt = jnp.clip((rem + MOE_TILE - 1) // MOE_TILE, 0, VISIT_ROWS // MOE_TILE)
    last = jnp.maximum(n_used - 1, 0)
    vis_nt = jnp.where(used, nt, 0).astype(I32)
    vis_cnt = jnp.where(used, jnp.clip(rem, 0, VISIT_ROWS), 0).astype(I32)
    vis_e = jnp.where(used, ve, ve[last]).astype(I32)
    return dest2, row_tok, vis_e, vis_nt, vis_cnt


def _final_kernel(posc_ref, posn_ref, x1_ref, gf_ref, w_ref, g_ref, ys_hbm, o_ref, ybuf, sem,
                  *, tm, nsteps):
    i = pl.program_id(0)
    slot = i % 2

    def row_copy(pos_ref, dst_slot, j):
        return pltpu.make_async_copy(ys_hbm.at[pl.ds(pos_ref[0, j], 1)],
                                     ybuf.at[dst_slot, pl.ds(j, 1)], sem.at[dst_slot])

    def issue(pos_ref, dst_slot):
        def body(j, c):
            row_copy(pos_ref, dst_slot, j).start()
            return c
        lax.fori_loop(0, 2 * tm, body, 0, unroll=8)

    @pl.when(i == 0)
    def _():
        issue(posc_ref, 0)

    @pl.when(i + 1 < nsteps)
    def _():
        issue(posn_ref, 1 - slot)

    pltpu.make_async_copy(ys_hbm.at[pl.ds(0, 2 * tm)], ybuf.at[slot], sem.at[slot]).wait()

    w = w_ref[...]
    moe = w[:, 0:1] * ybuf[slot, 0:tm, :] + w[:, 1:2] * ybuf[slot, tm:2 * tm, :]
    x = x1_ref[...] + gf_ref[...] * moe
    ms = jnp.mean(x * x, axis=-1, keepdims=True)
    o_ref[...] = x * lax.rsqrt(ms + EPS) * g_ref[...]


def _combine_final(x1, ys, pos, gate_w, mod3, g_final, seq):
    n, d = x1.shape
    tm = FINAL_ROWS
    nsteps = n // tm
    per_b = seq // tm
    pos3 = pos.reshape(2, nsteps, tm).transpose(1, 0, 2).reshape(nsteps, 1, 2 * tm)
    return pl.pallas_call(
        functools.partial(_final_kernel, tm=tm, nsteps=nsteps),
        out_shape=jax.ShapeDtypeStruct((n, d), F32),
        grid=(nsteps,),
        in_specs=[pl.BlockSpec((None, 1, 2 * tm), lambda i: (i, 0, 0), memory_space=pltpu.SMEM),
                  pl.BlockSpec((None, 1, 2 * tm), lambda i: (jnp.minimum(i + 1, nsteps - 1), 0, 0),
                               memory_space=pltpu.SMEM),
                  pl.BlockSpec((tm, d), lambda i: (i, 0)),
                  pl.BlockSpec((None, 1, d), lambda i: (i // per_b, 0, 5)),
                  pl.BlockSpec((tm, 2), lambda i: (i, 0)),
                  pl.BlockSpec((1, d), lambda i: (0, 0)),
                  pl.BlockSpec(memory_space=pl.ANY)],
        out_specs=pl.BlockSpec((tm, d), lambda i: (i, 0)),
        scratch_shapes=[pltpu.VMEM((2, 2 * tm, d), F32), pltpu.SemaphoreType.DMA((2,))],
        compiler_params=_params("arbitrary"),
        name="combine_final",
    )(pos3, pos3, x1, mod3, gate_w, g_final.reshape(1, d), ys)


def kernel(x, c, rel_bias_table, w_ada, b_ada, g_mix, w_in, lambda_q1, lambda_k1, lambda_q2,
           lambda_k2, g_subln, w_proj_a, w_proj_b, w_out, g_ffn, w_router_group, b_router_group,
           w_router_expert, b_router_expert, w_expert_gate, w_expert_up, w_expert_down, g_final):
    bsz, seq, d = x.shape
    n = bsz * seq
    depth = w_in.shape[0]
    assert depth == 1, "the MoE combine is fused with the final RMSNorm: one layer only"
    da_width = DA_HEADS * HEAD_DIM
    sb_width = SB_HEADS * HEAD_DIM
    qkv_cols = 3 * da_width + 3 * sb_width
    attn_tile = ATTN_TILE
    attn_qpt = seq // attn_tile
    n_exp = w_expert_gate.shape[1]
    xf = x.reshape(n, d)

    for l in range(depth):
        lam_init = 0.8 - 0.6 * math.exp(-0.3 * l)
        mod = _adaln_mod(c, w_ada[l:l + 1], b_ada[l])
        mod3 = mod.reshape(bsz, 1, N_MOD * d)

        h = _norm_modulate(xf.reshape(bsz, seq, d), g_mix[l], mod3, 1, 0).reshape(n, d)
        qkv = _in_proj(h, w_in[l:l + 1], 0, qkv_cols, BF16, gate=False)
        gates = _in_proj(h, w_in[l:l + 1], qkv_cols, 2 * d, F32, gate=True)
        qkv3 = qkv.reshape(bsz, seq, qkv_cols)

        lam_params = jnp.stack([lambda_q1[l], lambda_k1[l], lambda_q2[l], lambda_k2[l]]).astype(F32)
        bias = _bias_rows(rel_bias_table, attn_tile)
        oa = _diff_attention(qkv3, lam_params, g_subln[l], bias, lam_init, attn_tile, attn_qpt)
        ob = _sb_attention(qkv3, 3 * DA_HEADS, attn_tile, attn_qpt)

        w_r = jnp.concatenate([w_router_group[l], w_router_expert[l]], axis=1).astype(F32).T
        w_r = jnp.pad(w_r, ((0, LANES - w_r.shape[0]), (0, 0)))
        wrh = w_r.astype(BF16)
        wrl = (w_r - wrh.astype(F32)).astype(BF16)
        b_r = jnp.concatenate([b_router_group[l], b_router_expert[l]]).astype(F32)
        b_r = jnp.pad(b_r, (0, LANES - b_r.shape[0])).reshape(LANES, 1)
        merged = _merge_branches(oa.reshape(n, da_width), ob.reshape(n, sb_width), gates,
                                 w_proj_a[l].astype(BF16), w_proj_b[l].astype(BF16))
        x1, h2, logits_t = _post_attention(merged, xf, mod3, g_ffn[l], w_out[l].astype(BF16),
                                           wrh, wrl, b_r, seq)

        e2, g2 = _route(logits_t)
        pos, row_tok, vis_e, vis_nt, vis_cnt = _expert_layout(e2, n_exp)
        ys = _moe_experts(h2, row_tok, vis_e, vis_nt, vis_cnt,
                          w_expert_gate[l], w_expert_up[l], w_expert_down[l])
        out = _combine_final(x1, ys, pos, g2.T, mod3, g_final, seq)
    return out.reshape(bsz, seq, d)
```

```python
import functools
import math

import jax
import jax.numpy as jnp
from jax import lax
from jax.experimental import pallas as pl
from jax.experimental.pallas import tpu as pltpu

F32 = jnp.float32
BF16 = jnp.bfloat16
I32 = jnp.int32
EPS = 1e-6

DA_HEADS = 8
DA_HALF_DIM = 64
SB_HEADS = 8
HEAD_DIM = 128
REL_BUCKETS = 32
REL_MAX_DIST = 128
N_GROUPS = 8
EXPERTS_PER_GROUP = 8
N_MOD = 6

VMEM_LIMIT_BYTES = 56 * 1024 * 1024
LANES = 128

ADALN_COLS = 1024
NORM_ROWS = 512
PROJ_ROWS = 1024
PROJ_COLS = 1024
ATTN_TILE = 256
MERGE_ROWS = 512
POST_ROWS = 512
ROUTE_COLS = 1024
FINAL_ROWS = 256

NT_DIMS = (((1,), (1,)), ((), ()))


def _params(*sem):
    return pltpu.CompilerParams(dimension_semantics=sem, vmem_limit_bytes=VMEM_LIMIT_BYTES)


def _sigmoid(v):
    return 1.0 / (1.0 + jnp.exp(-v))


def _mod_kernel(c_ref, w_ref, b_ref, o_ref):
    c = c_ref[...]
    s = (c * _sigmoid(c)).astype(BF16)
    o_ref[...] = jnp.dot(s, w_ref[...].astype(BF16), preferred_element_type=F32) + b_ref[...]


def _adaln_mod(c, w_ada, b_ada):
    bsz, d = c.shape
    ncol = w_ada.shape[-1]
    tn = ADALN_COLS
    return pl.pallas_call(
        _mod_kernel,
        out_shape=jax.ShapeDtypeStruct((bsz, ncol), F32),
        grid=(ncol // tn,),
        in_specs=[pl.BlockSpec((bsz, d), lambda j: (0, 0)),
                  pl.BlockSpec((None, d, tn), lambda j: (0, 0, j)),
                  pl.BlockSpec((1, tn), lambda j: (0, j))],
        out_specs=pl.BlockSpec((bsz, tn), lambda j: (0, j)),
        compiler_params=_params("arbitrary"),
        name="adaln_mod",
    )(c, w_ada, b_ada.reshape(1, ncol))


def _hnorm_kernel(x_ref, g_ref, sc_ref, sh_ref, o_ref):
    x = x_ref[...]
    ms = jnp.mean(x * x, axis=-1, keepdims=True)
    y = x * lax.rsqrt(ms + EPS) * g_ref[...]
    o_ref[...] = (y * (1.0 + sc_ref[...]) + sh_ref[...]).astype(o_ref.dtype)


def _norm_modulate(x, g, mod3, scale_idx, shift_idx):
    bsz, s, d = x.shape
    ts = NORM_ROWS
    return pl.pallas_call(
        _hnorm_kernel,
        out_shape=jax.ShapeDtypeStruct((bsz, s, d), BF16),
        grid=(bsz, s // ts),
        in_specs=[pl.BlockSpec((None, ts, d), lambda b, i: (b, i, 0)),
                  pl.BlockSpec((1, d), lambda b, i: (0, 0)),
                  pl.BlockSpec((None, 1, d), lambda b, i: (b, 0, scale_idx)),
                  pl.BlockSpec((None, 1, d), lambda b, i: (b, 0, shift_idx))],
        out_specs=pl.BlockSpec((None, ts, d), lambda b, i: (b, i, 0)),
        compiler_params=_params("arbitrary", "arbitrary"),
        name="norm_modulate",
    )(x, g.reshape(1, d), mod3, mod3)


def _proj_kernel(h_ref, w_ref, o_ref, wb_ref, *, gate):
    @pl.when(pl.program_id(1) == 0)
    def _():
        wb_ref[...] = w_ref[...].astype(BF16)

    r = jnp.dot(h_ref[...], wb_ref[...], preferred_element_type=F32)
    if gate:
        r = _sigmoid(r)
    o_ref[...] = r.astype(o_ref.dtype)


def _in_proj(h2d, w_in, col0, ncols, out_dtype, gate):
    n, d = h2d.shape
    tn, tm = PROJ_COLS, (PROJ_ROWS if gate else 2 * PROJ_ROWS)
    jb = col0 // tn
    return pl.pallas_call(
        functools.partial(_proj_kernel, gate=gate),
        out_shape=jax.ShapeDtypeStruct((n, ncols), out_dtype),
        grid=(ncols // tn, n // tm),
        in_specs=[pl.BlockSpec((tm, d), lambda j, i: (i, 0)),
                  pl.BlockSpec((None, d, tn), lambda j, i: (0, 0, j + jb))],
        out_specs=pl.BlockSpec((tm, tn), lambda j, i: (i, j)),
        scratch_shapes=[pltpu.VMEM((d, tn), BF16)],
        compiler_params=_params("arbitrary", "arbitrary"),
        name="in_proj_gate" if gate else "in_proj_qkv",
    )(h2d, w_in)


def _rel_bucket(n):
    n = jnp.maximum(n, 0)
    max_exact = REL_BUCKETS // 2
    nf = jnp.maximum(n, 1).astype(F32)
    large = max_exact + (jnp.log(nf / max_exact) / math.log(REL_MAX_DIST / max_exact)
                         * (REL_BUCKETS - max_exact)).astype(I32)
    large = jnp.minimum(large, REL_BUCKETS - 1)
    return jnp.where(n < max_exact, n, large)


def _bias_rows(rel_table, t):
    assert 2 * t - (t - 1) >= REL_MAX_DIST, "far tiles must sit wholly in the last bucket"
    nb, h = rel_table.shape
    dist = jnp.arange(-(t - 1), 3 * t, dtype=I32)
    by_dist = rel_table[_rel_bucket(dist)].astype(F32).T
    rows = [jnp.pad(by_dist[:, delta * t:delta * t + 2 * t - 1][:, ::-1], ((0, 0), (0, 1)))
            for delta in range(2)]
    rows.append(jnp.broadcast_to(by_dist[:, -1:], (h, 2 * t)))
    return jnp.stack(rows, axis=1)


def _lane_chunks(x):
    return [x[:, c * LANES:(c + 1) * LANES] for c in range(x.shape[1] // LANES)]


def _diff_attn_kernel(lam_ref, g_ref, brow_ref, q_ref, k_ref, v_ref, o_ref, s1_ref, s2_ref, bias_ref,
                      *, t, nsteps, qpt, lam_init):
    step = pl.program_id(2)
    lp = lam_ref[...]
    lam = (jnp.exp(jnp.sum(lp[0:1] * lp[1:2], axis=-1, keepdims=True))
           - jnp.exp(jnp.sum(lp[2:3] * lp[3:4], axis=-1, keepdims=True)) + lam_init)
    lane = lax.broadcasted_iota(I32, (t, HEAD_DIM), 1)

    qrow = lax.broadcasted_iota(I32, (t, t), 0)
    kcol = lax.broadcasted_iota(I32, (t, t), 1)
    for delta in range(2):
        gen = jnp.broadcast_to(brow_ref[delta:delta + 1, :], (t, 2 * t))
        tile = pltpu.roll(gen, t + 1, 1, stride=1, stride_axis=0)[:, :t]
        if delta == 0:
            tile = jnp.where(kcol <= qrow, tile, -jnp.inf)
        bias_ref[delta] = tile
    bias_ref[2] = jnp.broadcast_to(brow_ref[2:3, :t], (t, t))

    def scores(qh, s_ref, nk):
        mx = None
        for j in range(nk):
            s = (lax.dot_general(qh, k_ref[j * t:(j + 1) * t, :], NT_DIMS, preferred_element_type=F32)
                 + bias_ref[min(nk - 1 - j, 2)])
            s_ref[:, j * t:(j + 1) * t] = s
            for ch in _lane_chunks(s):
                mx = ch if mx is None else jnp.maximum(mx, ch)
        return jnp.broadcast_to(jnp.max(mx, axis=-1, keepdims=True), (t, LANES))

    def softmax_pv(s_ref, mb, nk):
        lsum = jnp.zeros((t, LANES), F32)
        acc = jnp.zeros((t, HEAD_DIM), F32)
        for j in range(nk):
            ps = [jnp.exp(ch - mb) for ch in _lane_chunks(s_ref[:, j * t:(j + 1) * t])]
            for p in ps:
                lsum = lsum + p
            acc = acc + jnp.dot(jnp.concatenate(ps, axis=1).astype(BF16), v_ref[j * t:(j + 1) * t, :],
                                preferred_element_type=F32)
        return acc / jnp.sum(lsum, axis=-1, keepdims=True)

    def q_tile(lt, nk):
        rows = slice(lt * t, (lt + 1) * t)
        q = q_ref[rows, :] * jnp.asarray(DA_HALF_DIM ** -0.5, BF16)
        q1 = jnp.where(lane < DA_HALF_DIM, q, jnp.zeros_like(q))
        q2 = jnp.where(lane >= DA_HALF_DIM, q, jnp.zeros_like(q))
        s1, s2 = s1_ref.at[lt % 2], s2_ref.at[lt % 2]
        m1 = scores(q1, s1, nk)
        m2 = scores(q2, s2, nk)
        o = softmax_pv(s1, m1, nk) - lam * softmax_pv(s2, m2, nk)
        ms = jnp.mean(o * o, axis=-1, keepdims=True)
        o = (o * lax.rsqrt(ms + EPS) * g_ref[...]) * (1.0 - lam_init)
        o_ref[rows, :] = o.astype(o_ref.dtype)

    def variant(first_tile):
        for lt in range(qpt):
            q_tile(lt, first_tile + lt + 1)

    if nsteps == 1:
        variant(0)
    else:
        for i in range(nsteps):
            pl.when(step == i)(functools.partial(variant, i * qpt))


def _diff_attention(qkv3, lam_params, g_subln, bias, lam_init, t, qpt):
    bsz, s, _ = qkv3.shape
    h = DA_HEADS
    nsteps = s // (t * qpt)
    return pl.pallas_call(
        functools.partial(_diff_attn_kernel, t=t, nsteps=nsteps, qpt=qpt, lam_init=lam_init),
        out_shape=jax.ShapeDtypeStruct((bsz, s, h * HEAD_DIM), BF16),
        scratch_shapes=[pltpu.VMEM((2, t, s), F32), pltpu.VMEM((2, t, s), F32), pltpu.VMEM((3, t, t), F32)],
        grid=(bsz, h, nsteps),
        in_specs=[pl.BlockSpec((4, DA_HALF_DIM), lambda b, hh, i: (0, 0)),
                  pl.BlockSpec((1, HEAD_DIM), lambda b, hh, i: (0, 0)),
                  pl.BlockSpec((None, 3, 2 * t), lambda b, hh, i: (hh, 0, 0)),
                  pl.BlockSpec((None, t * qpt, HEAD_DIM), lambda b, hh, i: (b, i, hh)),
                  pl.BlockSpec((None, s, HEAD_DIM), lambda b, hh, i: (b, 0, h + hh)),
                  pl.BlockSpec((None, s, HEAD_DIM), lambda b, hh, i: (b, 0, 2 * h + hh))],
        out_specs=pl.BlockSpec((None, t * qpt, HEAD_DIM), lambda b, hh, i: (b, i, hh)),
        compiler_params=_params("arbitrary", "arbitrary", "arbitrary"),
        name="diff_attention",
    )(lam_params, g_subln.reshape(1, HEAD_DIM), bias, qkv3, qkv3, qkv3)


def _sb_attn_kernel(q_ref, k_ref, v_ref, o_ref, e_ref, *, t, nsteps, qpt, scale):
    step = pl.program_id(2)
    row = lax.broadcasted_iota(I32, (t, t), 0)
    col = lax.broadcasted_iota(I32, (t, t), 1)
    neg_tri = jnp.where(row > col, -1.0, 0.0).astype(BF16)
    past = col < row

    def q_tile(lt, nk):
        rows = slice(lt * t, (lt + 1) * t)
        q = q_ref[rows, :]
        e = e_ref.at[lt % 2]
        totals = []
        for j in range(nk):
            z = lax.dot_general(q, k_ref[j * t:(j + 1) * t, :], NT_DIMS, preferred_element_type=F32) * scale
            softplus = jnp.maximum(z, 0.0) + jnp.log(1.0 + jnp.exp2(jnp.abs(z) * (-math.log2(math.e))))
            log_beta = z - softplus
            if j == nk - 1:
                softplus = jnp.where(past, softplus, 0.0)
            excl = jnp.dot(softplus.astype(BF16), neg_tri, preferred_element_type=F32)
            e[:, j * t:(j + 1) * t] = log_beta + excl
            totals.append(excl[:, 0:1] - softplus[:, 0:1])

        acc = jnp.zeros((t, HEAD_DIM), F32)
        later = jnp.zeros((t, 1), F32)
        for j in reversed(range(nk)):
            lb = jnp.broadcast_to(later, (t, LANES))
            ws = [jnp.exp(ch + lb) for ch in _lane_chunks(e[:, j * t:(j + 1) * t])]
            w = jnp.concatenate(ws, axis=1)
            if j == nk - 1:
                w = jnp.where(past, w, 0.0)
            acc = acc + jnp.dot(w.astype(BF16), v_ref[j * t:(j + 1) * t, :], preferred_element_type=F32)
            later = later + totals[j]
        o_ref[rows, :] = acc.astype(o_ref.dtype)

    def variant(first_tile):
        for lt in range(qpt):
            q_tile(lt, first_tile + lt + 1)

    if nsteps == 1:
        variant(0)
    else:
        for i in range(nsteps):
            pl.when(step == i)(functools.partial(variant, i * qpt))


def _sb_attention(qkv3, col_block0, t, qpt):
    bsz, s, _ = qkv3.shape
    h = SB_HEADS
    nsteps = s // (t * qpt)
    return pl.pallas_call(
        functools.partial(_sb_attn_kernel, t=t, nsteps=nsteps, qpt=qpt, scale=HEAD_DIM ** -0.5),
        out_shape=jax.ShapeDtypeStruct((bsz, s, h * HEAD_DIM), BF16),
        scratch_shapes=[pltpu.VMEM((2, t, s), F32)],
        grid=(bsz, h, nsteps),
        in_specs=[pl.BlockSpec((None, t * qpt, HEAD_DIM), lambda b, hh, i: (b, i, col_block0 + hh)),
                  pl.BlockSpec((None, s, HEAD_DIM), lambda b, hh, i: (b, 0, col_block0 + h + hh)),
                  pl.BlockSpec((None, s, HEAD_DIM), lambda b, hh, i: (b, 0, col_block0 + 2 * h + hh))],
        out_specs=pl.BlockSpec((None, t * qpt, HEAD_DIM), lambda b, hh, i: (b, i, hh)),
        compiler_params=_params("arbitrary", "arbitrary", "arbitrary"),
        name="sb_attention",
    )(qkv3, qkv3, qkv3)


def _both_attn_kernel(lam_ref, g_ref, brow_ref, qa_ref, ka_ref, va_ref, qb_ref, kb_ref, vb_ref,
                      oa_ref, ob_ref, s1_ref, s2_ref, bias_ref, e_ref, *, t, qpt, lam_init, scale):
    _diff_attn_kernel(lam_ref, g_ref, brow_ref, qa_ref, ka_ref, va_ref, oa_ref, s1_ref, s2_ref, bias_ref,
                      t=t, nsteps=1, qpt=qpt, lam_init=lam_init)
    _sb_attn_kernel(qb_ref, kb_ref, vb_ref, ob_ref, e_ref, t=t, nsteps=1, qpt=qpt, scale=scale)


def _both_attention(qkv3, lam_params, g_subln, bias, lam_init, t):
    bsz, s, _ = qkv3.shape
    h = DA_HEADS
    assert SB_HEADS == DA_HEADS
    blk = lambda c0: pl.BlockSpec((None, s, HEAD_DIM), lambda b, hh, i, c0=c0: (b, 0, c0 + hh))
    out = jax.ShapeDtypeStruct((bsz, s, h * HEAD_DIM), BF16)
    return pl.pallas_call(
        functools.partial(_both_attn_kernel, t=t, qpt=s // t, lam_init=lam_init, scale=HEAD_DIM ** -0.5),
        out_shape=(out, out),
        scratch_shapes=[pltpu.VMEM((2, t, s), F32), pltpu.VMEM((2, t, s), F32), pltpu.VMEM((3, t, t), F32),
                        pltpu.VMEM((2, t, s), F32)],
        grid=(bsz, h, 1),
        in_specs=[pl.BlockSpec((4, DA_HALF_DIM), lambda b, hh, i: (0, 0)),
                  pl.BlockSpec((1, HEAD_DIM), lambda b, hh, i: (0, 0)),
                  pl.BlockSpec((None, 3, 2 * t), lambda b, hh, i: (hh, 0, 0)),
                  blk(0), blk(h), blk(2 * h), blk(3 * h), blk(4 * h), blk(5 * h)],
        out_specs=(blk(0), blk(0)),
        compiler_params=_params("arbitrary", "arbitrary", "arbitrary"),
        name="both_attention",
    )(lam_params, g_subln.reshape(1, HEAD_DIM), bias, qkv3, qkv3, qkv3, qkv3, qkv3, qkv3)


def _merge_kernel(oa_ref, ob_ref, sa_ref, sb_ref, wa_ref, wb_ref, m_ref):
    merged = (sa_ref[...] * jnp.dot(oa_ref[...], wa_ref[...], preferred_element_type=F32)
              + sb_ref[...] * jnp.dot(ob_ref[...], wb_ref[...], preferred_element_type=F32))
    m_ref[...] = merged.astype(m_ref.dtype)


def _merge_branches(oa, ob, gates, wa, wb):
    n, width = oa.shape
    d = wa.shape[1]
    tm = MERGE_ROWS
    const = lambda shape: pl.BlockSpec(shape, lambda i: (0,) * len(shape), pipeline_mode=pl.Buffered(1))
    return pl.pallas_call(
        _merge_kernel,
        out_shape=jax.ShapeDtypeStruct((n, d), BF16),
        grid=(n // tm,),
        in_specs=[pl.BlockSpec((tm, width), lambda i: (i, 0)),
                  pl.BlockSpec((tm, width), lambda i: (i, 0)),
                  pl.BlockSpec((tm, d), lambda i: (i, 0)),
                  pl.BlockSpec((tm, d), lambda i: (i, 1)),
                  const((width, d)), const((width, d))],
        out_specs=pl.BlockSpec((tm, d), lambda i: (i, 0)),
        compiler_params=_params("arbitrary"),
        name="merge_branches",
    )(oa, ob, gates, gates, wa, wb)


def _post_kernel(m_ref, x_ref, gm_ref, scf_ref, shf_ref, gffn_ref,
                 wo_ref, wrh_ref, wrl_ref, br_ref,
                 x1_ref, h2_ref, lg_ref):
    y = jnp.dot(m_ref[...], wo_ref[...], preferred_element_type=F32)
    x1 = x_ref[...] + gm_ref[...] * y
    x1_ref[...] = x1
    ms = jnp.mean(x1 * x1, axis=-1, keepdims=True)
    h2 = (x1 * lax.rsqrt(ms + EPS) * gffn_ref[...]) * (1.0 + scf_ref[...]) + shf_ref[...]
    h2_ref[...] = h2
    hb = h2.astype(BF16)
    hl = (h2 - hb.astype(F32)).astype(BF16)
    wrh = wrh_ref[...]
    lg = (lax.dot_general(wrh, hb, NT_DIMS, preferred_element_type=F32)
          + lax.dot_general(wrh, hl, NT_DIMS, preferred_element_type=F32)
          + lax.dot_general(wrl_ref[...], hb, NT_DIMS, preferred_element_type=F32))
    lg_ref[...] = lg + br_ref[...]


def _post_attention(merged, x2d, mod3, g_ffn, wo, wrh, wrl, br, seq):
    n, d = x2d.shape
    tm = POST_ROWS
    per_b = seq // tm
    const = lambda shape: pl.BlockSpec(shape, lambda i: (0,) * len(shape), pipeline_mode=pl.Buffered(1))
    modspec = lambda idx: pl.BlockSpec((None, 1, d), lambda i: (i // per_b, 0, idx))
    return pl.pallas_call(
        _post_kernel,
        out_shape=(jax.ShapeDtypeStruct((n, d), F32),
                   jax.ShapeDtypeStruct((n, d), F32),
                   jax.ShapeDtypeStruct((LANES, n), F32)),
        grid=(n // tm,),
        in_specs=[pl.BlockSpec((tm, d), lambda i: (i, 0)),
                  pl.BlockSpec((tm, d), lambda i: (i, 0)),
                  modspec(2), modspec(4), modspec(3),
                  const((1, d)),
                  const((d, d)),
                  const((LANES, d)), const((LANES, d)), const((LANES, 1))],
        out_specs=(pl.BlockSpec((tm, d), lambda i: (i, 0)),
                   pl.BlockSpec((tm, d), lambda i: (i, 0)),
                   pl.BlockSpec((LANES, tm), lambda i: (0, i))),
        compiler_params=_params("arbitrary"),
        name="post_attention",
    )(merged, x2d, mod3, mod3, mod3, g_ffn.reshape(1, d), wo, wrh, wrl, br)


def _first_index_of_max(vals, iota, nrows):
    mx = jnp.max(vals, axis=0, keepdims=True)
    idx = jnp.min(jnp.where(vals == mx, iota, nrows), axis=0, keepdims=True)
    return mx, idx


def _route_kernel(lg_ref, e_ref, w_ref):
    g = N_GROUPS
    epg = EXPERTS_PER_GROUP
    lg = lg_ref[...]
    gl = lg[0:g, :]
    iota = lax.broadcasted_iota(I32, gl.shape, 0)
    gmax, gidx = _first_index_of_max(gl, iota, g)
    p_g = 1.0 / jnp.sum(jnp.exp(gl - gmax), axis=0, keepdims=True)

    esel = jnp.zeros((epg, lg.shape[1]), F32)
    for gi in range(g):
        esel = jnp.where(gidx == gi, lg[g + gi * epg:g + (gi + 1) * epg, :], esel)
    emax = jnp.max(esel, axis=0, keepdims=True)
    ex = jnp.exp(esel - emax)
    prob = ex / jnp.sum(ex, axis=0, keepdims=True)

    p0, i0 = _first_index_of_max(prob, iota, epg)
    rest = jnp.where(iota == i0, -1.0, prob)
    p1, i1 = _first_index_of_max(rest, iota, epg)
    tot = p0 + p1
    e_ref[0:1, :] = gidx * epg + i0
    e_ref[1:2, :] = gidx * epg + i1
    w_ref[0:1, :] = p_g * (p0 / tot)
    w_ref[1:2, :] = p_g * (p1 / tot)


def _route(logits_t):
    rows, n = logits_t.shape
    tn = ROUTE_COLS
    return pl.pallas_call(
        _route_kernel,
        out_shape=(jax.ShapeDtypeStruct((2, n), I32), jax.ShapeDtypeStruct((2, n), F32)),
        grid=(n // tn,),
        in_specs=[pl.BlockSpec((rows, tn), lambda i: (0, i))],
        out_specs=(pl.BlockSpec((2, tn), lambda i: (0, i)), pl.BlockSpec((2, tn), lambda i: (0, i))),
        compiler_params=_params("arbitrary"),
        name="route",
    )(logits_t)


VISIT_ROWS = 512
MOE_TILE = 256
MOE_FCHUNK = 512
GATHER_CHUNK = 32


def _moe_kernel(ve_ref, vnt_ref, vcnt_ref,
                tokc_ref, tokn_ref, h_hbm, wg_ref, wu_ref, wd_ref,
                y_ref,
                xbuf, sem, wgb, wub, wdb, *, nf, nv):
    v = pl.program_id(0)
    f = pl.program_id(1)
    slot = v % 2
    nt = vnt_ref[v]
    chunks_per_step = VISIT_ROWS // nf // GATHER_CHUNK

    def n_chunks(vv):
        return lax.shift_right_logical(vcnt_ref[vv] + (GATHER_CHUNK - 1), GATHER_CHUNK.bit_length() - 1)

    def row_copy(tok_ref, dst_slot, r):
        tok = tok_ref[0, r]
        return pltpu.make_async_copy(h_hbm.at[pl.ds(tok, 1)], xbuf.at[dst_slot, pl.ds(r, 1)],
                                     sem.at[dst_slot])

    def issue_chunks(tok_ref, dst_slot, c0, n):
        def chunk(c, carry):
            base = (c0 + c) * GATHER_CHUNK

            def body(r, cc):
                row_copy(tok_ref, dst_slot, base + r).start()
                return cc
            lax.fori_loop(0, GATHER_CHUNK, body, 0, unroll=8)
            return carry
        lax.fori_loop(0, n, chunk, 0)

    def wait_chunks(dst_slot, n):
        def chunk(c, carry):
            pltpu.make_async_copy(h_hbm.at[pl.ds(0, GATHER_CHUNK)],
                                  xbuf.at[dst_slot, pl.ds(0, GATHER_CHUNK)], sem.at[dst_slot]).wait()
            return carry
        lax.fori_loop(0, n, chunk, 0)

    @pl.when(jnp.logical_and(v == 0, f == 0))
    def _():
        xbuf[...] = jnp.zeros_like(xbuf)
        issue_chunks(tokc_ref, 0, 0, n_chunks(0))

    @pl.when(jnp.logical_and(f == 0, nt > 0))
    def _():
        wait_chunks(slot, n_chunks(v))

    @pl.when(f == 0)
    def _():
        y_ref[...] = jnp.zeros_like(y_ref)

    nxt = jnp.minimum(v + 1, nv - 1)
    chunks_next = jnp.where(v + 1 < nv, n_chunks(nxt), 0)
    c0 = f * chunks_per_step
    issue_chunks(tokn_ref, 1 - slot, c0, jnp.clip(chunks_next - c0, 0, chunks_per_step))

    @pl.when(nt > 0)
    def _():
        wgb[...] = wg_ref[...].astype(BF16)
        wub[...] = wu_ref[...].astype(BF16)
        wdb[...] = wd_ref[...].astype(BF16)
        for tl in range(VISIT_ROWS // MOE_TILE):
            @pl.when(tl < nt)
            def _():
                rows = pl.ds(tl * MOE_TILE, MOE_TILE)
                x = xbuf[slot, rows, :].astype(BF16)
                a = jnp.dot(x, wgb[...], preferred_element_type=F32)
                u = jnp.dot(x, wub[...], preferred_element_type=F32)
                hmid = ((a * _sigmoid(a)) * u).astype(BF16)
                y_ref[rows, :] += jnp.dot(hmid, wdb[...], preferred_element_type=F32)


def _moe_experts(h2, row_tok, vis_e, vis_nt, vis_cnt, w_gate, w_up, w_down):
    n, d = h2.shape
    n_exp, _, dexp = w_gate.shape
    nv = vis_e.shape[0]
    nf = dexp // MOE_FCHUNK
    assert GATHER_CHUNK & (GATHER_CHUNK - 1) == 0 and VISIT_ROWS % (nf * GATHER_CHUNK) == 0
    tok3 = row_tok.reshape(nv, 1, VISIT_ROWS)

    def fidx(v, f, vnt):
        return jnp.where(vnt[v] > 0, f, nf - 1)

    grid_spec = pltpu.PrefetchScalarGridSpec(
        num_scalar_prefetch=3,
        grid=(nv, nf),
        in_specs=[
            pl.BlockSpec((None, 1, VISIT_ROWS), lambda v, f, ve, vnt, vcnt: (v, 0, 0),
                         memory_space=pltpu.SMEM),
            pl.BlockSpec((None, 1, VISIT_ROWS), lambda v, f, ve, vnt, vcnt: (jnp.minimum(v + 1, nv - 1), 0, 0),
                         memory_space=pltpu.SMEM),
            pl.BlockSpec(memory_space=pl.ANY),
            pl.BlockSpec((None, d, MOE_FCHUNK), lambda v, f, ve, vnt, vcnt: (ve[v], 0, fidx(v, f, vnt))),
            pl.BlockSpec((None, d, MOE_FCHUNK), lambda v, f, ve, vnt, vcnt: (ve[v], 0, fidx(v, f, vnt))),
            pl.BlockSpec((None, MOE_FCHUNK, d), lambda v, f, ve, vnt, vcnt: (ve[v], fidx(v, f, vnt), 0)),
        ],
        out_specs=pl.BlockSpec((VISIT_ROWS, d), lambda v, f, ve, vnt, vcnt: (v, 0)),
        scratch_shapes=[pltpu.VMEM((2, VISIT_ROWS, d), F32),
                        pltpu.SemaphoreType.DMA((2,)),
                        pltpu.VMEM((d, MOE_FCHUNK), BF16),
                        pltpu.VMEM((d, MOE_FCHUNK), BF16),
                        pltpu.VMEM((MOE_FCHUNK, d), BF16)],
    )
    return pl.pallas_call(
        functools.partial(_moe_kernel, nf=nf, nv=nv),
        out_shape=jax.ShapeDtypeStruct((nv * VISIT_ROWS, d), F32),
        grid_spec=grid_spec,
        compiler_params=_params("arbitrary", "arbitrary"),
        name="moe_experts",
    )(vis_e, vis_nt, vis_cnt, tok3, tok3, h2, w_gate, w_up, w_down)


RANK_BLOCK = 256


def _slot_kernel(e_ref, dest_ref, cnt_ref, rank_ref, *, n_exp):
    nk, n = e_ref.shape
    eids = lax.broadcasted_iota(I32, (n_exp, RANK_BLOCK), 0)
    row = lax.broadcasted_iota(I32, (RANK_BLOCK, RANK_BLOCK), 0)
    col = lax.broadcasted_iota(I32, (RANK_BLOCK, RANK_BLOCK), 1)
    before = (row < col).astype(BF16)
    carry = jnp.zeros((n_exp, 1), F32)
    for k in range(nk):
        for c0 in range(0, n, RANK_BLOCK):
            hit = eids == e_ref[k:k + 1, c0:c0 + RANK_BLOCK]
            onehot = jnp.where(hit, 1.0, 0.0)
            earlier = jnp.dot(onehot.astype(BF16), before, preferred_element_type=F32) + carry
            rank = jnp.sum(jnp.where(hit, earlier, 0.0), axis=0, keepdims=True)
            rank_ref[k:k + 1, c0:c0 + RANK_BLOCK] = rank.astype(I32)
            carry = carry + jnp.sum(onehot, axis=1, keepdims=True)
    cnt_ref[...] = jnp.broadcast_to(carry, cnt_ref.shape).astype(I32)

    n_vis = lax.shift_right_logical(carry.astype(I32) + (VISIT_ROWS - 1),
                                    VISIT_ROWS.bit_length() - 1).astype(F32)
    ei = lax.broadcasted_iota(I32, (n_exp, n_exp), 0)
    ej = lax.broadcasted_iota(I32, (n_exp, n_exp), 1)
    lower = (ej < ei).astype(BF16)
    first_row = jnp.dot(lower, jnp.broadcast_to(n_vis, (n_exp, LANES)).astype(BF16),
                        preferred_element_type=F32)[:, 0:1] * float(VISIT_ROWS)
    for k in range(nk):
        for c0 in range(0, n, RANK_BLOCK):
            hit = eids == e_ref[k:k + 1, c0:c0 + RANK_BLOCK]
            base = jnp.sum(jnp.where(hit, first_row, 0.0), axis=0, keepdims=True)
            dest_ref[k:k + 1, c0:c0 + RANK_BLOCK] = rank_ref[k:k + 1, c0:c0 + RANK_BLOCK] + base.astype(I32)


def _assignment_slots(e2, n_exp):
    nk, n = e2.shape
    assert (n // VISIT_ROWS * nk + n_exp) <= 256, "visit counts must stay exact in bf16"
    return pl.pallas_call(
        functools.partial(_slot_kernel, n_exp=n_exp),
        out_shape=(jax.ShapeDtypeStruct((nk, n), I32), jax.ShapeDtypeStruct((n_exp, LANES), I32)),
        scratch_shapes=[pltpu.VMEM((nk, n), I32)],
        compiler_params=pltpu.CompilerParams(vmem_limit_bytes=VMEM_LIMIT_BYTES),
        name="assignment_slots",
    )(e2)


def _invert_kernel(dest_ref, tok_ref, zeros_vmem, sem):
    nk, n = dest_ref.shape
    zeros_vmem[...] = jnp.zeros_like(zeros_vmem)
    fill = pltpu.make_async_copy(zeros_vmem, tok_ref, sem)
    fill.start()
    fill.wait()

    def body(t, c):
        for k in range(nk):
            tok_ref[dest_ref[k, t]] = t
        return c
    lax.fori_loop(0, n, body, 0, unroll=8)


def _slot_tokens(dest2, n_slots):
    return pl.pallas_call(
        _invert_kernel,
        out_shape=jax.ShapeDtypeStruct((n_slots,), I32),
        in_specs=[pl.BlockSpec(memory_space=pltpu.SMEM)],
        out_specs=pl.BlockSpec(memory_space=pltpu.SMEM),
        scratch_shapes=[pltpu.VMEM((n_slots,), I32), pltpu.SemaphoreType.DMA],
        name="slot_tokens",
    )(dest2)


def _expert_layout(e2, n_exp):
    n = e2.shape[1]
    m = 2 * n
    nv = m // VISIT_ROWS + n_exp
    dest2, cnt = _assignment_slots(e2, n_exp)
    counts = cnt[:, 0]
    n_vis = (counts + VISIT_ROWS - 1) // VISIT_ROWS
    cum_vis = jnp.cumsum(n_vis)
    vbase = cum_vis - n_vis
    row_tok = _slot_tokens(dest2, nv * VISIT_ROWS)

    n_used = cum_vis[-1]
    vid = jnp.arange(nv, dtype=I32)
    used = vid < n_used
    ve = jnp.minimum(jnp.sum((cum_vis[None, :] <= vid[:, None]).astype(I32), axis=1), n_exp - 1)
    rem = counts[ve] - (vid - vbase[ve]) * VISIT_ROWS
    nt = jnp.clip((rem + MOE_TILE - 1) // MOE_TILE, 0, VISIT_ROWS // MOE_TILE)
    last = jnp.maximum(n_used - 1, 0)
    vis_nt = jnp.where(used, nt, 0).astype(I32)
    vis_cnt = jnp.where(used, jnp.clip(rem, 0, VISIT_ROWS), 0).astype(I32)
    vis_e = jnp.where(used, ve, ve[last]).astype(I32)
    return dest2, row_tok, vis_e, vis_nt, vis_cnt


def _final_kernel(posc_ref, posn_ref, x1_ref, gf_ref, w_ref, g_ref, ys_hbm, o_ref, ybuf, sem,
                  *, tm, nsteps):
    i = pl.program_id(0)
    slot = i % 2

    def row_copy(pos_ref, dst_slot, j):
        return pltpu.make_async_copy(ys_hbm.at[pl.ds(pos_ref[0, j], 1)],
                                     ybuf.at[dst_slot, pl.ds(j, 1)], sem.at[dst_slot])

    def issue(pos_ref, dst_slot):
        def body(j, c):
            row_copy(pos_ref, dst_slot, j).start()
            return c
        lax.fori_loop(0, 2 * tm, body, 0, unroll=8)

    @pl.when(i == 0)
    def _():
        issue(posc_ref, 0)

    @pl.when(i + 1 < nsteps)
    def _():
        issue(posn_ref, 1 - slot)

    pltpu.make_async_copy(ys_hbm.at[pl.ds(0, 2 * tm)], ybuf.at[slot], sem.at[slot]).wait()

    w = w_ref[...]
    moe = w[:, 0:1] * ybuf[slot, 0:tm, :] + w[:, 1:2] * ybuf[slot, tm:2 * tm, :]
    x = x1_ref[...] + gf_ref[...] * moe
    ms = jnp.mean(x * x, axis=-1, keepdims=True)
    o_ref[...] = x * lax.rsqrt(ms + EPS) * g_ref[...]


def _combine_final(x1, ys, pos, gate_w, mod3, g_final, seq):
    n, d = x1.shape
    tm = FINAL_ROWS
    nsteps = n // tm
    per_b = seq // tm
    pos3 = pos.reshape(2, nsteps, tm).transpose(1, 0, 2).reshape(nsteps, 1, 2 * tm)
    return pl.pallas_call(
        functools.partial(_final_kernel, tm=tm, nsteps=nsteps),
        out_shape=jax.ShapeDtypeStruct((n, d), F32),
        grid=(nsteps,),
        in_specs=[pl.BlockSpec((None, 1, 2 * tm), lambda i: (i, 0, 0), memory_space=pltpu.SMEM),
                  pl.BlockSpec((None, 1, 2 * tm), lambda i: (jnp.minimum(i + 1, nsteps - 1), 0, 0),
                               memory_space=pltpu.SMEM),
                  pl.BlockSpec((tm, d), lambda i: (i, 0)),
                  pl.BlockSpec((None, 1, d), lambda i: (i // per_b, 0, 5)),
                  pl.BlockSpec((tm, 2), lambda i: (i, 0)),
                  pl.BlockSpec((1, d), lambda i: (0, 0)),
                  pl.BlockSpec(memory_space=pl.ANY)],
        out_specs=pl.BlockSpec((tm, d), lambda i: (i, 0)),
        scratch_shapes=[pltpu.VMEM((2, 2 * tm, d), F32), pltpu.SemaphoreType.DMA((2,))],
        compiler_params=_params("arbitrary"),
        name="combine_final",
    )(pos3, pos3, x1, mod3, gate_w, g_final.reshape(1, d), ys)


def kernel(x, c, rel_bias_table, w_ada, b_ada, g_mix, w_in, lambda_q1, lambda_k1, lambda_q2,
           lambda_k2, g_subln, w_proj_a, w_proj_b, w_out, g_ffn, w_router_group, b_router_group,
           w_router_expert, b_router_expert, w_expert_gate, w_expert_up, w_expert_down, g_final):
    bsz, seq, d = x.shape
    n = bsz * seq
    depth = w_in.shape[0]
    assert depth == 1, "the MoE combine is fused with the final RMSNorm: one layer only"
    da_width = DA_HEADS * HEAD_DIM
    sb_width = SB_HEADS * HEAD_DIM
    qkv_cols = 3 * da_width + 3 * sb_width
    attn_tile = ATTN_TILE
    attn_qpt = seq // attn_tile
    n_exp = w_expert_gate.shape[1]
    xf = x.reshape(n, d)

    for l in range(depth):
        lam_init = 0.8 - 0.6 * math.exp(-0.3 * l)
        mod = _adaln_mod(c, w_ada[l:l + 1], b_ada[l])
        mod3 = mod.reshape(bsz, 1, N_MOD * d)

        h = _norm_modulate(xf.reshape(bsz, seq, d), g_mix[l], mod3, 1, 0).reshape(n, d)
        qkv = _in_proj(h, w_in[l:l + 1], 0, qkv_cols, BF16, gate=False)
        gates = _in_proj(h, w_in[l:l + 1], qkv_cols, 2 * d, F32, gate=True)
        qkv3 = qkv.reshape(bsz, seq, qkv_cols)

        lam_params = jnp.stack([lambda_q1[l], lambda_k1[l], lambda_q2[l], lambda_k2[l]]).astype(F32)
        bias = _bias_rows(rel_bias_table, attn_tile)
        oa, ob = _both_attention(qkv3, lam_params, g_subln[l], bias, lam_init, attn_tile)

        w_r = jnp.concatenate([w_router_group[l], w_router_expert[l]], axis=1).astype(F32).T
        w_r = jnp.pad(w_r, ((0, LANES - w_r.shape[0]), (0, 0)))
        wrh = w_r.astype(BF16)
        wrl = (w_r - wrh.astype(F32)).astype(BF16)
        b_r = jnp.concatenate([b_router_group[l], b_router_expert[l]]).astype(F32)
        b_r = jnp.pad(b_r, (0, LANES - b_r.shape[0])).reshape(LANES, 1)
        merged = _merge_branches(oa.reshape(n, da_width), ob.reshape(n, sb_width), gates,
                                 w_proj_a[l].astype(BF16), w_proj_b[l].astype(BF16))
        x1, h2, logits_t = _post_attention(merged, xf, mod3, g_ffn[l], w_out[l].astype(BF16),
                                           wrh, wrl, b_r, seq)

        e2, g2 = _route(logits_t)
        pos, row_tok, vis_e, vis_nt, vis_cnt = _expert_layout(e2, n_exp)
        ys = _moe_experts(h2, row_tok, vis_e, vis_nt, vis_cnt,
                          w_expert_gate[l], w_expert_up[l], w_expert_down[l])
        out = _combine_final(x1, ys, pos, g2.T, mod3, g_final, seq)
    return out.reshape(bsz, seq, d)
```
